```python
import math
import jax, jax.numpy as jnp
from jax import lax
import numpy as np

D_MODEL = 2048
BATCH = 1
SEQ = 16384
DEPTH = 1
DEC_BATCH = 32
DEC_SEQ = 8
PAST_LEN = 16384
PAGE_SIZE = 128

D_MLSTM = D_MODEL // 2
D_NSA = D_MODEL - D_MLSTM
M_HEADS = 8
M_HD = D_MLSTM // M_HEADS
CONV_W = 4
MLSTM_CHUNK = 64
A_HEADS = 8
A_HD = D_NSA // A_HEADS
KV_HEADS = 2
GROUP = A_HEADS // KV_HEADS
CMP_BLOCK = 32
CMP_STRIDE = 16
CMP_HIDDEN = 2 * A_HD
SEL_BLOCK = 64
N_SELECT = 16
WINDOW = 512
Q_BLOCK = 128
ATTN_SCALE = A_HD ** -0.5
D_FF = 4 * D_MODEL
EPS = 1e-6
NEG = -1e30
BIG = 1e9
Q_END = D_NSA
KV_END = Q_END + 6 * KV_HEADS * A_HD
G_END = KV_END + 3 * A_HEADS
QK_END = G_END + 2 * D_MLSTM
V_END = QK_END + D_MLSTM
IF_END = V_END + 2 * M_HEADS
D_IN = IF_END + D_MLSTM

kernel_name = 'hymba_mlstm_nsa_decode_step'


def _rms(x, g):
    xf = x.astype(jnp.float32)
    y = xf * lax.rsqrt(jnp.mean(xf * xf, axis=-1, keepdims=True) + EPS)
    return (y * g.astype(jnp.float32)).astype(x.dtype)


def _masked_softmax(s, mask, axis):
    p = jax.nn.softmax(jnp.where(mask, s, NEG), axis=axis)
    return jnp.where(mask, p, 0.0)


def _split_proj(h, w_in):
    B, T = h.shape[:2]
    z = h @ w_in
    aq, akv, ag, mqk, mv, mif, mo = jnp.split(z, [Q_END, KV_END, G_END, QK_END, V_END, IF_END], axis=-1)
    return aq, akv.reshape(B, T, 6, KV_HEADS, A_HD), ag, mqk, mv, mif, mo


def _causal_conv(u, prev, w, b):
    T = u.shape[1]
    up = jnp.concatenate([prev.astype(u.dtype), u], axis=1)
    y = up[:, 0:T] * w[0]
    for j in range(1, CONV_W):
        y = y + up[:, j:j + T] * w[j]
    return jax.nn.silu(y + b), up[:, T:]


def _mlstm_chunk(carry, inp):
    C, n, m = carry
    q, k, v, ig, lf = inp
    L = q.shape[2]
    F = jnp.cumsum(lf, axis=-1)
    tri = jnp.tril(jnp.ones((L, L), dtype=bool))
    D = jnp.where(tri, ig[..., None, :] + F[..., :, None] - F[..., None, :], NEG)
    a = m[..., None] + F
    mt = jnp.maximum(a, jnp.max(D, axis=-1))
    S = jnp.einsum('bhtd,bhsd->bhts', q, k) * jnp.exp(D - mt[..., None])
    inter = jnp.exp(a - mt)
    num = jnp.einsum('bhts,bhsd->bhtd', S, v) + inter[..., None] * jnp.einsum('bhtd,bhde->bhte', q, C)
    den = jnp.sum(S, axis=-1) + inter * jnp.einsum('bhtd,bhd->bht', q, n)
    h = num / jnp.maximum(jnp.abs(den), jnp.exp(-mt))[..., None]
    m_new = mt[..., -1]
    w_s = jnp.exp(ig + F[..., -1:] - F - m_new[..., None])
    decay = jnp.exp(a[..., -1] - m_new)
    C_new = decay[..., None, None] * C + jnp.einsum('bhs,bhsd,bhse->bhde', w_s, k, v)
    n_new = decay[..., None] * n + jnp.einsum('bhs,bhsd->bhd', w_s, k)
    return (C_new, n_new, m_new), h


def _mlstm(q, k, v, ig, lf, C, n, m):
    B, T, H, _ = q.shape
    L = math.gcd(T, MLSTM_CHUNK)
    nc = T // L

    def to_chunks(a):
        a = a.reshape((B, nc, L) + a.shape[2:])
        return jnp.moveaxis(jnp.moveaxis(a, 1, 0), 2, 3)

    (C, n, m), h = lax.scan(_mlstm_chunk, (C, n, m),
                            (to_chunks(q), to_chunks(k), to_chunks(v), to_chunks(ig), to_chunks(lf)))
    h = jnp.moveaxis(jnp.moveaxis(h, 3, 2), 0, 1).reshape(B, T, H, M_HD)
    return h, C, n, m


def _mlstm_branch(mqk, mv, mif, mo, conv_prev, C0, n0, m0, lw):
    B, T = mqk.shape[:2]
    qk, conv_new = _causal_conv(mqk, conv_prev, lw['w_conv'], lw['b_conv'])
    q, k = jnp.split(qk.astype(jnp.float32), 2, axis=-1)
    q = q.reshape(B, T, M_HEADS, M_HD)
    k = k.reshape(B, T, M_HEADS, M_HD) * (M_HD ** -0.5)
    v = mv.astype(jnp.float32).reshape(B, T, M_HEADS, M_HD)
    ig = mif[..., :M_HEADS].astype(jnp.float32) + lw['b_i']
    lf = jax.nn.log_sigmoid(mif[..., M_HEADS:].astype(jnp.float32) + lw['b_f'])
    h, C, n, m = _mlstm(q, k, v, ig, lf, C0.astype(jnp.float32), n0.astype(jnp.float32), m0.astype(jnp.float32))
    h = _rms(h, lw['g_mnorm']) * jax.nn.sigmoid(mo.astype(jnp.float32)).reshape(B, T, M_HEADS, M_HD)
    return h.reshape(B, T, D_MLSTM).astype(mqk.dtype), conv_new, C, n, m


def _compress(k, w1, w2, pe):
    B, L, G, hd = k.shape
    P = L // CMP_STRIDE
    pieces = k[:, :P * CMP_STRIDE].reshape(B, P, CMP_STRIDE, G, hd)
    w1r = w1.reshape(CMP_BLOCK // CMP_STRIDE, CMP_STRIDE, hd, CMP_HIDDEN)
    lo = jnp.einsum('bpjgd,jdh->bpgh', pieces, w1r[0])
    hi = jnp.einsum('bpjgd,jdh->bpgh', pieces, w1r[1])
    hid = lo[:, :-1] + hi[:, 1:] + pe.reshape(-1) @ w1
    return jax.nn.silu(hid) @ w2


def _compress_kv(k_raw, v_raw, lw):
    kc = _rms(_compress(k_raw, lw['w_phi1_k'], lw['w_phi2_k'], lw['pe_k']), lw['g_kc'])
    vc = _compress(v_raw, lw['w_phi1_v'], lw['w_phi2_v'], lw['pe_v'])
    return kc, vc


def _cmp_to_sel(imp, n_sel):
    nc = imp.shape[-1]
    r = SEL_BLOCK // CMP_STRIDE
    lead = CMP_BLOCK // CMP_STRIDE - 1
    padded = jnp.pad(imp, [(0, 0)] * (imp.ndim - 1) + [(lead, r * (n_sel + 1) - lead - nc)])
    P = padded.reshape(imp.shape[:-1] + (n_sel + 1, r))
    return jnp.sum(P[..., :n_sel, :], axis=-1) + jnp.sum(P[..., 1:, :lead], axis=-1)


def _nsa_inputs(aq, akv, ag, lw):
    B, T = aq.shape[:2]
    q = (_rms(aq.reshape(B, T, A_HEADS, A_HD), lw['g_q']) * ATTN_SCALE).reshape(B, T, KV_HEADS, GROUP, A_HD)
    gates = ag.reshape(B, T, KV_HEADS, GROUP, 3)
    kc, vc, ks, vs, kw, vw = [akv[:, :, j] for j in range(6)]
    return q, gates, kc, vc, _rms(ks, lw['g_ks']), vs, _rms(kw, lw['g_kw']), vw


def _nsa_core(q, qpos, kc, vc, cend, fetch, kw, vw, kwpos, gates, n_sel):
    s = jnp.einsum('btgrd,bcgd->btgrc', q, kc).astype(jnp.float32)
    cmask = (cend[None, :] <= qpos[:, None])[None, :, None, None, :]
    p = _masked_softmax(s, cmask, -1)
    o_cmp = jnp.einsum('btgrc,bcgd->btgrd', p.astype(vc.dtype), vc)
    imp = _cmp_to_sel(jnp.sum(p, axis=3), n_sel)
    blk = jnp.arange(n_sel)[None, :]
    cur = (qpos // SEL_BLOCK)[:, None]
    forced = (blk == 0) | (blk == cur) | (blk == cur - 1)
    score = jnp.where((blk <= cur)[None, :, None, :], jnp.where(forced[None, :, None, :], BIG, imp), NEG)
    _, idx = lax.top_k(score, min(N_SELECT, n_sel))
    ks, vs = fetch(idx)
    kpos = idx[..., None] * SEL_BLOCK + jnp.arange(SEL_BLOCK)
    smask = (kpos <= qpos[None, :, None, None, None])[:, :, :, None]
    s = jnp.einsum('btgrd,btgnkd->btgrnk', q, ks).astype(jnp.float32)
    p = _masked_softmax(s, smask, (-2, -1))
    o_slc = jnp.einsum('btgrnk,btgnkd->btgrd', p.astype(vs.dtype), vs)
    s = jnp.einsum('btgrd,bsgd->btgrs', q, kw).astype(jnp.float32)
    rel = qpos[:, None] - kwpos[None, :]
    wmask = ((rel >= 0) & (rel < WINDOW) & (kwpos[None, :] >= 0))[None, :, None, None, :]
    p = _masked_softmax(s, wmask, -1)
    o_win = jnp.einsum('btgrs,bsgd->btgrd', p.astype(vw.dtype), vw)
    g = jax.nn.sigmoid(gates)
    return g[..., 0:1] * o_cmp + g[..., 1:2] * o_slc + g[..., 2:3] * o_win


def _nsa_prompt(aq, akv, ag, lw):
    B, T = aq.shape[:2]
    q, gates, kc_raw, vc_raw, ks, vs, kw, vw = _nsa_inputs(aq, akv, ag, lw)
    kc, vc = _compress_kv(kc_raw, vc_raw, lw)
    cend = jnp.arange(kc.shape[1]) * CMP_STRIDE + (CMP_BLOCK - 1)
    n_sel = T // SEL_BLOCK
    kb = ks.reshape(B, n_sel, SEL_BLOCK, KV_HEADS, A_HD)
    vb = vs.reshape(B, n_sel, SEL_BLOCK, KV_HEADS, A_HD)
    bi = jnp.arange(B)[:, None, None, None, None]
    ki = jnp.arange(SEL_BLOCK)
    gi = jnp.arange(KV_HEADS)[:, None, None]

    def fetch(idx):
        j = idx[..., None]
        return kb[bi, j, ki, gi], vb[bi, j, ki, gi]

    kw_pad = jnp.pad(kw, ((0, 0), (WINDOW, 0), (0, 0), (0, 0)))
    vw_pad = jnp.pad(vw, ((0, 0), (WINDOW, 0), (0, 0), (0, 0)))

    def block(i):
        q0 = i * Q_BLOCK
        qpos = q0 + jnp.arange(Q_BLOCK)
        kwpos = q0 - WINDOW + jnp.arange(WINDOW + Q_BLOCK)
        qb = lax.dynamic_slice_in_dim(q, q0, Q_BLOCK, axis=1)
        gb = lax.dynamic_slice_in_dim(gates, q0, Q_BLOCK, axis=1)
        kwb = lax.dynamic_slice_in_dim(kw_pad, q0, WINDOW + Q_BLOCK, axis=1)
        vwb = lax.dynamic_slice_in_dim(vw_pad, q0, WINDOW + Q_BLOCK, axis=1)
        return _nsa_core(qb, qpos, kc, vc, cend, fetch, kwb, vwb, kwpos, gb, n_sel)

    out = lax.map(block, jnp.arange(T // Q_BLOCK))
    out = jnp.moveaxis(out, 0, 1).reshape(B, T, D_NSA)
    wb = min(WINDOW, T)
    return out, (kc_raw, vc_raw, ks, vs, kw[:, T - wb:], vw[:, T - wb:])


def _nsa_sample(aq, akv, ag, ck_cmp, cv_cmp, ck_slc, cv_slc, ck_win, cv_win, page_table, lw):
    B, T = aq.shape[:2]
    q, gates, kc_new, vc_new, ks_new, vs_new, kw_new, vw_new = _nsa_inputs(aq, akv, ag, lw)
    past = page_table.shape[1] * PAGE_SIZE

    def gather_rows(pool, new):
        rows = pool[page_table].reshape(B, past, KV_HEADS, A_HD)
        return jnp.concatenate([rows.astype(new.dtype), new], axis=1)

    kc, vc = _compress_kv(gather_rows(ck_cmp, kc_new), gather_rows(cv_cmp, vc_new), lw)
    cend = jnp.arange(kc.shape[1]) * CMP_STRIDE + (CMP_BLOCK - 1)
    n_past = past // SEL_BLOCK
    n_tail = -(-T // SEL_BLOCK)
    bpp = PAGE_SIZE // SEL_BLOCK
    pk = ck_slc.reshape(-1, SEL_BLOCK, KV_HEADS, A_HD)
    pv = cv_slc.reshape(-1, SEL_BLOCK, KV_HEADS, A_HD)

    def tail_blocks(new):
        return jnp.pad(new, ((0, 0), (0, n_tail * SEL_BLOCK - T), (0, 0), (0, 0))).reshape(
            B, n_tail, SEL_BLOCK, KV_HEADS, A_HD)

    tk, tv = tail_blocks(ks_new), tail_blocks(vs_new)
    bi = jnp.arange(B)[:, None, None, None]
    ki = jnp.arange(SEL_BLOCK)
    gi = jnp.arange(KV_HEADS)[:, None, None]

    def fetch(idx):
        in_past = (idx < n_past)[..., None, None]
        jp = jnp.minimum(idx, n_past - 1)
        phys = (page_table[bi, jp // bpp] * bpp + jp % bpp)[..., None]
        jt = jnp.clip(idx - n_past, 0, n_tail - 1)[..., None]
        bt = bi[..., None]
        k_sel = jnp.where(in_past, pk[phys, ki, gi].astype(tk.dtype), tk[bt, jt, ki, gi])
        v_sel = jnp.where(in_past, pv[phys, ki, gi].astype(tv.dtype), tv[bt, jt, ki, gi])
        return k_sel, v_sel

    kw = jnp.concatenate([ck_win.astype(kw_new.dtype), kw_new], axis=1)
    vw = jnp.concatenate([cv_win.astype(vw_new.dtype), vw_new], axis=1)
    wb = ck_win.shape[1]
    kwpos = past - wb + jnp.arange(wb + T)
    qpos = past + jnp.arange(T)
    out = _nsa_core(q, qpos, kc, vc, cend, fetch, kw, vw, kwpos, gates, n_past + n_tail)
    return out.reshape(B, T, D_NSA), (kc_new, vc_new, ks_new, vs_new, kw[:, T:], vw[:, T:])


def _channel(x, a_out, m_out, lw):
    x = x + jnp.concatenate([a_out.astype(x.dtype), m_out.astype(x.dtype)], axis=-1) @ lw['w_out']
    h = _rms(x, lw['g_mlp'])
    return x + jnp.square(jax.nn.relu(h @ lw['w_up'])) @ lw['w_down']


def _prompt_layer(x, lw):
    B = x.shape[0]
    aq, akv, ag, mqk, mv, mif, mo = _split_proj(_rms(x, lw['g_attn']), lw['w_in'])
    a_out, nsa_state = _nsa_prompt(aq, akv, ag, lw)
    conv0 = jnp.zeros((B, CONV_W - 1, 2 * D_MLSTM), x.dtype)
    C0 = jnp.zeros((B, M_HEADS, M_HD, M_HD), jnp.float32)
    n0 = jnp.zeros((B, M_HEADS, M_HD), jnp.float32)
    m0 = jnp.zeros((B, M_HEADS), jnp.float32)
    m_out, conv_new, C, n, m = _mlstm_branch(mqk, mv, mif, mo, conv0, C0, n0, m0, lw)
    return _channel(x, a_out, m_out, lw), nsa_state + (conv_new, C, n, m)


def _sample_layer(x, ck_cmp, cv_cmp, ck_slc, cv_slc, ck_win, cv_win, s_conv, s_C, s_n, s_m, page_table, lw):
    aq, akv, ag, mqk, mv, mif, mo = _split_proj(_rms(x, lw['g_attn']), lw['w_in'])
    a_out, nsa_state = _nsa_sample(aq, akv, ag, ck_cmp, cv_cmp, ck_slc, cv_slc, ck_win, cv_win, page_table, lw)
    m_out, conv_new, C, n, m = _mlstm_branch(mqk, mv, mif, mo, s_conv, s_C, s_n, s_m, lw)
    return _channel(x, a_out, m_out, lw), nsa_state + (
        conv_new.astype(s_conv.dtype), C.astype(s_C.dtype), n.astype(s_n.dtype), m.astype(s_m.dtype))


def setup_inputs(seed: int = 0) -> dict:
    key = jax.random.key(seed)
    ks = jax.random.split(key, 36)

    def nrm(i, shape, scale):
        return jax.random.normal(ks[i], shape, jnp.float32) * scale

    L = DEPTH
    n_pages = PAST_LEN // PAGE_SIZE
    n_pool = (5 * DEC_BATCH * n_pages) // 4
    wb = min(WINDOW, PAST_LEN)
    pool_shape = (L, n_pool, PAGE_SIZE, KV_HEADS, A_HD)
    win_shape = (L, DEC_BATCH, wb, KV_HEADS, A_HD)
    page_table = jax.random.permutation(ks[35], n_pool)[:DEC_BATCH * n_pages].reshape(
        DEC_BATCH, n_pages).astype(jnp.int32)
    return {
        'x_prompt': nrm(0, (BATCH, SEQ, D_MODEL), 1.0),
        'x_sample': nrm(1, (DEC_BATCH, DEC_SEQ, D_MODEL), 1.0),
        'cache_k_cmp': nrm(2, pool_shape, 1.0),
        'cache_v_cmp': nrm(3, pool_shape, 1.0),
        'cache_k_slc': nrm(4, pool_shape, 1.0),
        'cache_v_slc': nrm(5, pool_shape, 1.0),
        'cache_k_win': nrm(6, win_shape, 1.0),
        'cache_v_win': nrm(7, win_shape, 1.0),
        'state_conv': nrm(8, (L, DEC_BATCH, CONV_W - 1, 2 * D_MLSTM), 1.0),
        'state_C': nrm(9, (L, DEC_BATCH, M_HEADS, M_HD, M_HD), 0.3),
        'state_n': nrm(10, (L, DEC_BATCH, M_HEADS, M_HD), 0.3),
        'state_m': nrm(11, (L, DEC_BATCH, M_HEADS), 0.5),
        'page_table': page_table,
        'g_attn': 1.0 + nrm(12, (L, D_MODEL), 0.02),
        'w_in': nrm(13, (L, D_MODEL, D_IN), D_MODEL ** -0.5),
        'w_conv': nrm(14, (L, CONV_W, 2 * D_MLSTM), CONV_W ** -0.5),
        'b_conv': nrm(15, (L, 2 * D_MLSTM), 0.02),
        'b_i': nrm(16, (L, M_HEADS), 0.1),
        'b_f': 3.0 + nrm(17, (L, M_HEADS), 0.5),
        'g_mnorm': 1.0 + nrm(18, (L, M_HEADS, M_HD), 0.02),
        'g_q': 1.0 + nrm(19, (L, A_HD), 0.02),
        'g_ks': 1.0 + nrm(20, (L, A_HD), 0.02),
        'g_kw': 1.0 + nrm(21, (L, A_HD), 0.02),
        'g_kc': 1.0 + nrm(22, (L, A_HD), 0.02),
        'w_phi1_k': nrm(23, (L, CMP_BLOCK * A_HD, CMP_HIDDEN), (CMP_BLOCK * A_HD) ** -0.5),
        'w_phi2_k': nrm(24, (L, CMP_HIDDEN, A_HD), CMP_HIDDEN ** -0.5),
        'pe_k': nrm(25, (L, CMP_BLOCK, A_HD), 0.1),
        'w_phi1_v': nrm(26, (L, CMP_BLOCK * A_HD, CMP_HIDDEN), (CMP_BLOCK * A_HD) ** -0.5),
        'w_phi2_v': nrm(27, (L, CMP_HIDDEN, A_HD), CMP_HIDDEN ** -0.5),
        'pe_v': nrm(28, (L, CMP_BLOCK, A_HD), 0.1),
        'w_out': nrm(29, (L, D_MODEL, D_MODEL), D_MODEL ** -0.5),
        'g_mlp': 1.0 + nrm(30, (L, D_MODEL), 0.02),
        'w_up': nrm(31, (L, D_MODEL, D_FF), D_MODEL ** -0.5),
        'w_down': nrm(32, (L, D_FF, D_MODEL), D_FF ** -0.5),
    }


def reference(x_prompt, x_sample, cache_k_cmp, cache_v_cmp, cache_k_slc, cache_v_slc, cache_k_win,
              cache_v_win, state_conv, state_C, state_n, state_m, page_table, g_attn, w_in, w_conv,
              b_conv, b_i, b_f, g_mnorm, g_q, g_ks, g_kw, g_kc, w_phi1_k, w_phi2_k, pe_k, w_phi1_v,
              w_phi2_v, pe_v, w_out, g_mlp, w_up, w_down):
    xp, xs = x_prompt, x_sample
    new_p, new_s = [], []
    for l in range(DEPTH):
        lw = {'g_attn': g_attn[l], 'w_in': w_in[l], 'w_conv': w_conv[l], 'b_conv': b_conv[l],
              'b_i': b_i[l], 'b_f': b_f[l], 'g_mnorm': g_mnorm[l], 'g_q': g_q[l], 'g_ks': g_ks[l],
              'g_kw': g_kw[l], 'g_kc': g_kc[l], 'w_phi1_k': w_phi1_k[l], 'w_phi2_k': w_phi2_k[l],
              'pe_k': pe_k[l], 'w_phi1_v': w_phi1_v[l], 'w_phi2_v': w_phi2_v[l], 'pe_v': pe_v[l],
              'w_out': w_out[l], 'g_mlp': g_mlp[l], 'w_up': w_up[l], 'w_down': w_down[l]}
        xp, st_p = _prompt_layer(xp, lw)
        xs, st_s = _sample_layer(xs, cache_k_cmp[l], cache_v_cmp[l], cache_k_slc[l], cache_v_slc[l],
                                 cache_k_win[l], cache_v_win[l], state_conv[l], state_C[l], state_n[l],
                                 state_m[l], page_table, lw)
        new_p.append(st_p)
        new_s.append(st_s)
    (k_cmp_p, v_cmp_p, k_slc_p, v_slc_p, k_win_p, v_win_p, conv_p, C_p, n_p, m_p) = [
        jnp.stack(z) for z in zip(*new_p)]
    (k_cmp_s, v_cmp_s, k_slc_s, v_slc_s, k_win_s, v_win_s, conv_s, C_s, n_s, m_s) = [
        jnp.stack(z) for z in zip(*new_s)]
    return (xp, xs, k_cmp_p, v_cmp_p, k_slc_p, v_slc_p, k_win_p, v_win_p, conv_p, C_p, n_p, m_p,
            k_cmp_s, v_cmp_s, k_slc_s, v_slc_s, k_win_s, v_win_s, conv_s, C_s, n_s, m_s)
```

```python
import functools
import math

import jax
import jax.numpy as jnp
from jax import lax
from jax.experimental import pallas as pl
from jax.experimental.pallas import tpu as pltpu

F32 = jnp.float32
BF16 = jnp.bfloat16

EPS = 1e-6
NEG = -1e30
BIG = 1e9

M_HEADS = 8
A_HEADS = 8
KV_HEADS = 2
GROUP = A_HEADS // KV_HEADS
HD = 128
CONV_W = 4
CMP_BLOCK = 32
CMP_STRIDE = 16
CMP_HIDDEN = 2 * HD
SEL_BLOCK = 64
N_SELECT = 16
WINDOW = 512
PAGE_SIZE = 128

LANES = 128
VMEM_LIMIT_BYTES = 56 * 1024 * 1024

COL_MQK = 0
COL_Q = 16
COL_MV = 24
COL_MO = 32
COL_KC = 40
COL_VC = 42
COL_KS = 44
COL_VS = 46
COL_KW = 48
COL_VW = 50
COL_SMALL = 52
N_PACKED = 56 * LANES
GATE_LANE = 0
IG_LANE = 24
FG_LANE = 32


def _cparams(*sem):
    return pltpu.CompilerParams(dimension_semantics=sem, vmem_limit_bytes=VMEM_LIMIT_BYTES)


def _nt(a, b):
    return lax.dot_general(a, b, (((1,), (1,)), ((), ())), preferred_element_type=F32)


def _split3(x):
    x1 = x.astype(BF16)
    r1 = x - x1.astype(F32)
    x2 = r1.astype(BF16)
    r2 = r1 - x2.astype(F32)
    return x1, x2, r2.astype(BF16)


def _inproj_kernel(x_ref, g_ref, w_ref, gain_ref, flag_ref, z32_ref, z16_ref, h_ref, *, norm_tiles):
    j = pl.program_id(1)

    @pl.when(j == 0)
    def _():
        x = x_ref[...]
        ms = jnp.mean(x * x, axis=-1, keepdims=True)
        h_ref[...] = (x * lax.rsqrt(ms + EPS) * g_ref[...]).astype(BF16)

    z = jnp.dot(h_ref[...], w_ref[...], preferred_element_type=F32)
    is_norm = functools.reduce(jnp.logical_or, [j == t for t in norm_tiles])

    @pl.when(is_norm)
    def _():
        tn = z.shape[1]
        for c in range(tn // LANES):
            sl = slice(c * LANES, (c + 1) * LANES)
            zc = z[:, sl]
            nrm = zc * lax.rsqrt(jnp.mean(zc * zc, axis=-1, keepdims=True) + EPS) * gain_ref[:, sl]
            out = jnp.where(flag_ref[:, sl] > 0.5, nrm, zc)
            z32_ref[:, sl] = out
            z16_ref[:, sl] = out.astype(BF16)

    @pl.when(jnp.logical_not(is_norm))
    def _():
        z32_ref[...] = z
        z16_ref[...] = z.astype(BF16)


def _inproj(x, g, w, gain, flag, *, tm, tn=512, norm_tiles=(4, 5, 11, 12)):
    M, K = x.shape
    N = w.shape[1]
    return pl.pallas_call(
        functools.partial(_inproj_kernel, norm_tiles=norm_tiles),
        grid=(M // tm, N // tn),
        in_specs=[
            pl.BlockSpec((tm, K), lambda i, j: (i, 0)),
            pl.BlockSpec((1, K), lambda i, j: (0, 0)),
            pl.BlockSpec((K, tn), lambda i, j: (0, j)),
            pl.BlockSpec((1, tn), lambda i, j: (0, j)),
            pl.BlockSpec((1, tn), lambda i, j: (0, j)),
        ],
        out_specs=[
            pl.BlockSpec((tm, tn), lambda i, j: (i, j)),
            pl.BlockSpec((tm, tn), lambda i, j: (i, j)),
        ],
        out_shape=[jax.ShapeDtypeStruct((M, N), F32), jax.ShapeDtypeStruct((M, N), BF16)],
        scratch_shapes=[pltpu.VMEM((tm, K), BF16)],
        compiler_params=_cparams("arbitrary", "arbitrary"),
        name="inproj",
    )(x, g, w, gain, flag)


def _outproj_kernel(x_ref, a_ref, m_ref, wa_ref, wm_ref, o_ref):
    acc = jnp.dot(a_ref[...].astype(BF16), wa_ref[...], preferred_element_type=F32)
    acc = acc + jnp.dot(m_ref[...].astype(BF16), wm_ref[...], preferred_element_type=F32)
    o_ref[...] = x_ref[...] + acc


def _outproj(x, a, m, wa, wm, *, tm, tn=1024):
    M, N = x.shape
    Ka, Km = a.shape[1], m.shape[1]
    return pl.pallas_call(
        _outproj_kernel,
        grid=(M // tm, N // tn),
        in_specs=[
            pl.BlockSpec((tm, tn), lambda i, j: (i, j)),
            pl.BlockSpec((tm, Ka), lambda i, j: (i, 0)),
            pl.BlockSpec((tm, Km), lambda i, j: (i, 0)),
            pl.BlockSpec((Ka, tn), lambda i, j: (0, j)),
            pl.BlockSpec((Km, tn), lambda i, j: (0, j)),
        ],
        out_specs=pl.BlockSpec((tm, tn), lambda i, j: (i, j)),
        out_shape=jax.ShapeDtypeStruct((M, N), F32),
        compiler_params=_cparams("arbitrary", "arbitrary"),
        name="outproj",
    )(x, a, m, wa, wm)


def _mlp_kernel(x_ref, g_ref, wu_ref, wd_ref, o_ref, h_ref, acc_ref):
    k = pl.program_id(1)

    @pl.when(k == 0)
    def _():
        x = x_ref[...]
        ms = jnp.mean(x * x, axis=-1, keepdims=True)
        h_ref[...] = (x * lax.rsqrt(ms + EPS) * g_ref[...]).astype(BF16)
        acc_ref[...] = jnp.zeros_like(acc_ref)

    u = jnp.dot(h_ref[...], wu_ref[...], preferred_element_type=F32)
    u = jnp.square(jnp.maximum(u, 0.0)).astype(BF16)
    acc_ref[...] += jnp.dot(u, wd_ref[...], preferred_element_type=F32)

    @pl.when(k == pl.num_programs(1) - 1)
    def _():
        o_ref[...] = x_ref[...] + acc_ref[...]


def _mlp(x, g, wu, wd, *, tm, tf=512):
    M, D = x.shape
    FF = wu.shape[1]
    return pl.pallas_call(
        _mlp_kernel,
        grid=(M // tm, FF // tf),
        in_specs=[
            pl.BlockSpec((tm, D), lambda i, k: (i, 0)),
            pl.BlockSpec((1, D), lambda i, k: (0, 0)),
            pl.BlockSpec((D, tf), lambda i, k: (0, k)),
            pl.BlockSpec((tf, D), lambda i, k: (k, 0)),
        ],
        out_specs=pl.BlockSpec((tm, D), lambda i, k: (i, 0)),
        out_shape=jax.ShapeDtypeStruct((M, D), F32),
        scratch_shapes=[pltpu.VMEM((tm, D), BF16), pltpu.VMEM((tm, D), F32)],
        compiler_params=_cparams("arbitrary", "arbitrary"),
        name="mlp",
    )(x, g, wu, wd)


def _silu(x):
    return x * jax.nn.sigmoid(x)


def _log_sigmoid(x):
    return jnp.minimum(x, 0.0) - jnp.log1p(jnp.exp(-jnp.abs(x)))


def _mlstm_kernel(mqk_ref, mv_ref, sm_ref, mo_ref, conv0_ref, C0_ref, n0_ref, m0_ref,
                  wconv_ref, bconv_ref, gbias_ref, sel_ref, gm_ref,
                  out_ref, conv_ref, C_ref, n_ref, m_ref,
                  xbuf_ref, *pad_refs, L, valid):
    c = pl.program_id(1)
    dm = M_HEADS * HD

    @pl.when(c == 0)
    def _():
        C_ref[...] = C0_ref[...]
        n_ref[...] = n0_ref[...]
        m_ref[...] = m0_ref[...]
        xbuf_ref[...] = jnp.zeros_like(xbuf_ref)
        xbuf_ref[8 - (CONV_W - 1):8, :] = conv0_ref[0]

    if valid < L:
        vbuf, sbuf, obuf = pad_refs
        vbuf[...] = jnp.zeros_like(vbuf)
        sbuf[...] = jnp.zeros_like(sbuf)
        obuf[...] = jnp.zeros_like(obuf)
        vbuf[0:valid, :] = mv_ref[0]
        sbuf[0:valid, :] = sm_ref[0]
        obuf[0:valid, :] = mo_ref[0]
        xbuf_ref[8:8 + valid, :] = mqk_ref[0]
        v_all, sm, mo = vbuf[...], sbuf[...], obuf[...]
    else:
        xbuf_ref[8:8 + L, :] = mqk_ref[0]
        v_all, sm, mo = mv_ref[0], sm_ref[0], mo_ref[0]

    w = wconv_ref[...]
    y = bconv_ref[...] + xbuf_ref[8:8 + L, :] * w[CONV_W - 1:CONV_W, :]
    for j in range(CONV_W - 1):
        y = y + xbuf_ref[5 + j:5 + j + L, :] * w[j:j + 1, :]
    qk = _silu(y)
    tail = xbuf_ref[5 + valid:8 + valid, :]
    conv_ref[0] = tail
    xbuf_ref[5:8, :] = tail

    row = lax.broadcasted_iota(jnp.int32, (L, 1), 0)
    live = row < valid
    gt = sm + gbias_ref[...]
    ig_all = jnp.where(live, gt, NEG)
    lf_all = jnp.where(live, _log_sigmoid(gt), 0.0)
    F_all = lf_all
    sh = 1
    while sh < L:
        F_all = F_all + jnp.where(row >= sh, pltpu.roll(F_all, sh, axis=0), 0.0)
        sh *= 2
    sel = sel_ref[...]
    head_row = lax.broadcasted_iota(jnp.int32, (2 * M_HEADS, 1), 0)
    rows = jnp.zeros((2 * M_HEADS, L), F32)
    for a, b in zip(_split3(ig_all), _split3(F_all)):
        rows = rows + jnp.where(head_row < M_HEADS, _nt(sel, a), _nt(sel, b))

    tt = lax.broadcasted_iota(jnp.int32, (L, L), 0)
    ss = lax.broadcasted_iota(jnp.int32, (L, L), 1)
    tri = ss <= tt
    for h in range(M_HEADS):
        hs = slice(h * HD, (h + 1) * HD)
        F_c = F_all[:, FG_LANE + h:FG_LANE + h + 1]
        ig_c = ig_all[:, IG_LANE + h:IG_LANE + h + 1]
        ig_r = rows[h:h + 1, :]
        F_r = rows[M_HEADS + h:M_HEADS + h + 1, :]
        m_h = m_ref[0, :, h:h + 1]
        D = jnp.where(tri, ig_r + F_c - F_r, NEG)
        a_c = m_h + F_c
        mt = jnp.maximum(a_c, jnp.max(D, axis=-1, keepdims=True))
        qf = qk[:, hs]
        kf = qk[:, dm + h * HD:dm + (h + 1) * HD] * (HD ** -0.5)
        qb, kb = qf.astype(BF16), kf.astype(BF16)
        vb = v_all[:, hs].astype(BF16)
        S = _nt(qb, kb) * jnp.exp(D - mt)
        inter = jnp.exp(a_c - mt)
        Ch = C_ref[0, h]
        nh = n_ref[0, h:h + 1, :]
        num = jnp.dot(S.astype(BF16), vb, preferred_element_type=F32) + inter * jnp.dot(
            qb, Ch.astype(BF16), preferred_element_type=F32)
        den = jnp.sum(S, axis=-1, keepdims=True) + inter * jnp.sum(qf * nh, axis=-1, keepdims=True)
        hh = num / jnp.maximum(jnp.abs(den), jnp.exp(-mt))
        m_new = mt[valid - 1:valid, :]
        F_last = F_c[valid - 1:valid, :]
        w_c = jnp.exp(ig_c + F_last - F_c - m_new)
        decay = jnp.exp(a_c[valid - 1:valid, :] - m_new)
        kw = kf * w_c
        C_ref[0, h] = decay * Ch + lax.dot_general(
            kw.astype(BF16), vb, (((0,), (0,)), ((), ())), preferred_element_type=F32)
        n_ref[0, h:h + 1, :] = decay * nh + jnp.sum(kw, axis=0, keepdims=True)
        m_ref[0, :, h:h + 1] = m_new
        hn = hh * lax.rsqrt(jnp.mean(hh * hh, axis=-1, keepdims=True) + EPS) * gm_ref[h:h + 1, :]
        res = (hn * jax.nn.sigmoid(mo[:, hs])).astype(out_ref.dtype)
        out_ref[0, :, hs] = res[0:valid, :] if valid < L else res


def _mlstm(z32, z16, conv0, C0, n0, m0, wconv, bconv, gbias, sel, gm, *, B, T, L, valid):
    dm = M_HEADS * HD
    rows = valid if valid < L else L
    nchunk = T // rows
    z32 = z32.reshape(B, T, N_PACKED)
    z16 = z16.reshape(B, T, N_PACKED)
    scratch = [pltpu.VMEM((L + 8, 2 * dm), F32)]
    if valid < L:
        scratch += [pltpu.VMEM((L, dm), F32), pltpu.VMEM((L, LANES), F32), pltpu.VMEM((L, dm), F32)]
    zv = z32 if valid < L else z16
    out_dtype = F32 if valid < L else BF16
    const = lambda *shape: pl.BlockSpec(shape, lambda b, c: (0,) * len(shape))
    return pl.pallas_call(
        functools.partial(_mlstm_kernel, L=L, valid=valid),
        grid=(B, nchunk),
        in_specs=[
            pl.BlockSpec((1, rows, 2 * dm), lambda b, c: (b, c, COL_MQK * LANES // (2 * dm))),
            pl.BlockSpec((1, rows, dm), lambda b, c: (b, c, COL_MV * LANES // dm)),
            pl.BlockSpec((1, rows, LANES), lambda b, c: (b, c, COL_SMALL)),
            pl.BlockSpec((1, rows, dm), lambda b, c: (b, c, COL_MO * LANES // dm)),
            pl.BlockSpec((1, CONV_W - 1, 2 * dm), lambda b, c: (b, 0, 0)),
            pl.BlockSpec((1, M_HEADS, HD, HD), lambda b, c: (b, 0, 0, 0)),
            pl.BlockSpec((1, M_HEADS, HD), lambda b, c: (b, 0, 0)),
            pl.BlockSpec((1, 1, M_HEADS), lambda b, c: (b, 0, 0)),
            const(CONV_W, 2 * dm), const(1, 2 * dm), const(1, LANES), const(2 * M_HEADS, LANES),
            const(M_HEADS, HD),
        ],
        out_specs=[
            pl.BlockSpec((1, rows, dm), lambda b, c: (b, c, 0)),
            pl.BlockSpec((1, CONV_W - 1, 2 * dm), lambda b, c: (b, 0, 0)),
            pl.BlockSpec((1, M_HEADS, HD, HD), lambda b, c: (b, 0, 0, 0)),
            pl.BlockSpec((1, M_HEADS, HD), lambda b, c: (b, 0, 0)),
            pl.BlockSpec((1, 1, M_HEADS), lambda b, c: (b, 0, 0)),
        ],
        out_shape=[
            jax.ShapeDtypeStruct((B, T, dm), out_dtype),
            jax.ShapeDtypeStruct((B, CONV_W - 1, 2 * dm), F32),
            jax.ShapeDtypeStruct((B, M_HEADS, HD, HD), F32),
            jax.ShapeDtypeStruct((B, M_HEADS, HD), F32),
            jax.ShapeDtypeStruct((B, 1, M_HEADS), F32),
        ],
        scratch_shapes=scratch,
        compiler_params=_cparams("arbitrary", "arbitrary"),
        name="mlstm",
    )(z32, zv, z32, z32, conv0, C0, n0, m0.reshape(B, 1, M_HEADS), wconv, bconv, gbias, sel, gm)


def _compress_kernel(pt_ref, *refs, G, normalize):
    pages = refs[:G]
    w1_ref, cb_ref, w2_ref, gain_ref, out_ref, carry_ref = refs[G:]
    pg = pl.program_id(1)
    R = G * (PAGE_SIZE // CMP_STRIDE)

    @pl.when(pg == 0)
    def _():
        carry_ref[...] = jnp.zeros_like(carry_ref)

    X = jnp.concatenate([p[0] for p in pages], axis=0).astype(BF16)
    xs = []
    for g in range(KV_HEADS):
        xs.append(jnp.concatenate(
            [X[:, (j * KV_HEADS + g) * HD:(j * KV_HEADS + g + 1) * HD] for j in range(CMP_STRIDE)], axis=1))
    X2 = jnp.concatenate(xs, axis=0)
    LH = jnp.dot(X2, w1_ref[...], preferred_element_type=F32)
    row = lax.broadcasted_iota(jnp.int32, (R, 1), 0)
    for g in range(KV_HEADS):
        lo = LH[g * R:(g + 1) * R, :CMP_HIDDEN]
        hi = LH[g * R:(g + 1) * R, CMP_HIDDEN:]
        lo_prev = jnp.where(row == 0, carry_ref[g:g + 1, :], pltpu.roll(lo, 1, axis=0))
        carry_ref[g:g + 1, :] = lo[R - 1:R, :]
        hid = lo_prev + hi + cb_ref[0:1, :]
        out = jnp.dot(_silu(hid).astype(BF16), w2_ref[...], preferred_element_type=F32)
        if normalize:
            out = out * lax.rsqrt(jnp.mean(out * out, axis=-1, keepdims=True) + EPS) * gain_ref[...]
        out_ref[0, :, g * HD:(g + 1) * HD] = out.astype(out_ref.dtype)


def _compress(pool, page_table, w1cat, cbias, w2, gain, *, normalize, G=16):
    B, NP = page_table.shape
    ppp = PAGE_SIZE // CMP_STRIDE
    row_w = CMP_STRIDE * KV_HEADS * HD
    R = G * ppp

    def page_spec(k):
        return pl.BlockSpec((1, ppp, row_w), lambda b, pg, pt: (pt[b, pg * G + k], 0, 0))

    const = lambda *shape: pl.BlockSpec(shape, lambda b, pg, pt: (0,) * len(shape))
    return pl.pallas_call(
        functools.partial(_compress_kernel, G=G, normalize=normalize),
        grid_spec=pltpu.PrefetchScalarGridSpec(
            num_scalar_prefetch=1,
            grid=(B, NP // G),
            in_specs=[page_spec(k) for k in range(G)] + [
                const(CMP_STRIDE * HD, 2 * CMP_HIDDEN), const(8, CMP_HIDDEN), const(CMP_HIDDEN, HD),
                const(1, HD)],
            out_specs=pl.BlockSpec((1, R, KV_HEADS * HD), lambda b, pg, pt: (b, pg, 0)),
            scratch_shapes=[pltpu.VMEM((8, CMP_HIDDEN), F32)],
        ),
        out_shape=jax.ShapeDtypeStruct((B, NP * ppp, KV_HEADS * HD), BF16),
        compiler_params=_cparams("arbitrary", "arbitrary"),
        name="compress",
    )(page_table, *([pool] * G), w1cat, cbias, w2, gain)


def _pe_bias_kernel(pe_ref, w1_ref, o_ref):
    o_ref[...] = jnp.dot(pe_ref[...].astype(BF16), w1_ref[...].astype(BF16), preferred_element_type=F32)


def _pe_bias(pe, w1):
    pe8 = jnp.broadcast_to(pe.reshape(1, -1), (8, pe.size))
    return pl.pallas_call(
        _pe_bias_kernel,
        out_shape=jax.ShapeDtypeStruct((8, w1.shape[1]), F32),
        compiler_params=pltpu.CompilerParams(vmem_limit_bytes=VMEM_LIMIT_BYTES),
        name="pe_bias",
    )(pe8, w1)


def _softmax_rows(s, ok):
    sm = jnp.where(ok, s, NEG)
    mx = jnp.max(sm, axis=-1, keepdims=True)
    e = jnp.where(ok, jnp.exp(sm - mx), 0.0)
    den = jnp.sum(e, axis=-1, keepdims=True)
    return e / jnp.where(den > 0.0, den, 1.0)


def _topk_mask(score, blk, n_pick):
    width = score.shape[-1]
    sel = jnp.zeros(score.shape, F32)
    for _ in range(n_pick):
        mx = jnp.max(score, axis=-1, keepdims=True)
        idx = jnp.min(jnp.where(score == mx, blk, width), axis=-1, keepdims=True)
        hit = blk == idx
        sel = jnp.where(hit, 1.0, sel)
        score = jnp.where(hit, -jnp.inf, score)
    return sel


def _stack_heads(qf):
    return jnp.concatenate([qf[:, r * HD:(r + 1) * HD] for r in range(GROUP)], axis=0)


def _tile_rows(x):
    return jnp.concatenate([x] * GROUP, axis=0)


def _nsa_prompt_kernel(q_ref, kc_ref, vc_ref, ks_ref, vs_ref, *rest, tq, tk, nwin):
    kw_refs = rest[:nwin]
    vw_refs = rest[nwin:2 * nwin]
    gate_ref, A_ref, E_ref, o_ref = rest[2 * nwin:]
    g = pl.program_id(0)
    i = pl.program_id(1)
    q0 = i * tq
    q = _stack_heads(q_ref[...])
    qpos = q0 + lax.broadcasted_iota(jnp.int32, (tq, 1), 0)
    qpos4 = _tile_rows(qpos)

    kc = kc_ref[0]
    ncr = kc.shape[0]
    cidx = lax.broadcasted_iota(jnp.int32, (1, ncr), 1)
    cok = (cidx >= 1) & (CMP_STRIDE * cidx + (CMP_BLOCK - CMP_STRIDE - 1) <= qpos4)
    p = _softmax_rows(_nt(q, kc), cok)
    o_cmp = jnp.dot(p.astype(BF16), vc_ref[0], preferred_element_type=F32)
    psum = p[0:tq]
    for r in range(1, GROUP):
        psum = psum + p[r * tq:(r + 1) * tq]
    imp = jnp.zeros((tq, A_ref.shape[1]), F32)
    for part in _split3(psum):
        imp = imp + jnp.dot(part, A_ref[...], preferred_element_type=F32)
    nblk = A_ref.shape[1]
    blk = lax.broadcasted_iota(jnp.int32, (1, nblk), 1)
    cur = qpos // SEL_BLOCK
    forced = (blk == 0) | (blk == cur) | (blk == cur - 1)
    score = jnp.where(blk <= cur, jnp.where(forced, BIG, imp), NEG)
    sel = _topk_mask(score, blk, N_SELECT)

    bpt = tk // SEL_BLOCK
    tiles_per_vreg = LANES // bpt
    kiota = lax.broadcasted_iota(jnp.int32, (1, tk), 1)

    def body(kt, carry):
        m_i, l_i, acc = carry
        start = pl.multiple_of(kt * tk, tk)
        k = ks_ref[pl.ds(start, tk), :]
        v = vs_ref[pl.ds(start, tk), :]
        s = _nt(q, k)
        half = sel[:, 0:LANES]
        for c in range(1, nblk // LANES):
            half = jnp.where(kt // tiles_per_vreg == c, sel[:, c * LANES:(c + 1) * LANES], half)
        shift = (LANES - bpt * (kt % tiles_per_vreg)) % LANES
        moved = pltpu.roll(half, shift, axis=1)
        selx = jnp.dot(moved.astype(BF16), E_ref[...], preferred_element_type=F32)
        ok = (selx > 0.5) & (start + kiota <= qpos)
        s = jnp.where(_tile_rows(ok), s, 2.0 * NEG)
        m_new = jnp.maximum(m_i, jnp.max(s, axis=-1, keepdims=True))
        alpha = jnp.exp(m_i - m_new)
        pe = jnp.exp(s - m_new)
        l_new = alpha * l_i + jnp.sum(pe, axis=-1, keepdims=True)
        acc = alpha * acc + jnp.dot(pe.astype(BF16), v, preferred_element_type=F32)
        return m_new, l_new, acc

    nkt = (q0 + tq + tk - 1) // tk
    init = (jnp.full((GROUP * tq, 1), NEG, F32), jnp.zeros((GROUP * tq, 1), F32),
            jnp.zeros((GROUP * tq, HD), F32))
    _, l_f, acc_f = lax.fori_loop(0, nkt, body, init)
    o_slc = acc_f / l_f

    s_w, ok_w = [], []
    for j in range(nwin):
        kwpos = q0 + (j - (nwin - 1)) * tq + lax.broadcasted_iota(jnp.int32, (1, tq), 1)
        rel = qpos4 - kwpos
        ok_w.append((rel >= 0) & (rel < WINDOW) & (kwpos >= 0))
        s_w.append(_nt(q, kw_refs[j][...]))
    pw = _softmax_rows(jnp.concatenate(s_w, axis=1), jnp.concatenate(ok_w, axis=1))
    o_win = jnp.zeros((GROUP * tq, HD), F32)
    for j in range(nwin):
        o_win = o_win + jnp.dot(pw[:, j * tq:(j + 1) * tq].astype(BF16), vw_refs[j][...],
                                preferred_element_type=F32)

    gt = jax.nn.sigmoid(gate_ref[...])
    gt = jnp.where(g == 0, gt, pltpu.roll(gt, LANES - 3 * GROUP, axis=1))
    for r in range(GROUP):
        rs = slice(r * tq, (r + 1) * tq)
        o = (gt[:, 3 * r:3 * r + 1] * o_cmp[rs] + gt[:, 3 * r + 1:3 * r + 2] * o_slc[rs]
             + gt[:, 3 * r + 2:3 * r + 3] * o_win[rs])
        o_ref[:, r * HD:(r + 1) * HD] = o.astype(o_ref.dtype)


def _nsa_prompt(z32, z16, kc, vc, A, E, *, T, tq=128, tk=512):
    nwin = WINDOW // tq + 1
    ncr = kc.shape[1]
    gw = GROUP * HD

    def win_spec(col, j):
        return pl.BlockSpec((tq, HD), lambda g, i: (jnp.maximum(i + j - (nwin - 1), 0), col + g))

    const = lambda *shape: pl.BlockSpec(shape, lambda g, i: (0,) * len(shape))
    in_specs = [
        pl.BlockSpec((tq, gw), lambda g, i: (i, COL_Q * LANES // gw + g)),
        pl.BlockSpec((1, ncr, HD), lambda g, i: (0, 0, g)),
        pl.BlockSpec((1, ncr, HD), lambda g, i: (0, 0, g)),
        pl.BlockSpec((T, HD), lambda g, i: (0, COL_KS + g)),
        pl.BlockSpec((T, HD), lambda g, i: (0, COL_VS + g)),
    ]
    in_specs += [win_spec(COL_KW, j) for j in range(nwin)]
    in_specs += [win_spec(COL_VW, j) for j in range(nwin)]
    in_specs += [pl.BlockSpec((tq, LANES), lambda g, i: (i, COL_SMALL)), const(*A.shape), const(*E.shape)]
    return pl.pallas_call(
        functools.partial(_nsa_prompt_kernel, tq=tq, tk=tk, nwin=nwin),
        grid=(KV_HEADS, T // tq),
        in_specs=in_specs,
        out_specs=pl.BlockSpec((tq, gw), lambda g, i: (i, g)),
        out_shape=jax.ShapeDtypeStruct((T, KV_HEADS * gw), BF16),
        compiler_params=_cparams("arbitrary", "arbitrary"),
        name="nsa_prompt",
    )(z16, kc, vc, z16, z16, *([z16] * (2 * nwin)), z32, A, E)


def _nsa_sample_kernel(pt_ref, *refs, G, T, past, wb):
    kpages = refs[:G]
    vpages = refs[G:2 * G]
    (z_q, z_ks, z_vs, z_kw, z_vw, z_gate, kc_ref, vc_ref, kwin_ref, vwin_ref, A_ref, E_ref,
     o_ref, kwo_ref, vwo_ref,
     sel_ref, ocmp_ref, m_ref, l_ref, acc_ref, pad_ref) = refs[2 * G:]
    kt = pl.program_id(1)
    nkt = pl.num_programs(1)
    tk = G * PAGE_SIZE
    R = GROUP * T
    gw = GROUP * HD
    t_col = lax.broadcasted_iota(jnp.int32, (T, 1), 0)
    qpos4 = _tile_rows(past + t_col)
    qall = z_q[...].astype(BF16)
    qs = [_stack_heads(qall[:, g * gw:(g + 1) * gw]) for g in range(KV_HEADS)]

    @pl.when(kt == 0)
    def _():
        nblk = A_ref.shape[1]
        blk = lax.broadcasted_iota(jnp.int32, (1, nblk), 1)
        cur4 = qpos4 // SEL_BLOCK
        forced = (blk == 0) | (blk == cur4) | (blk == cur4 - 1)
        for g in range(KV_HEADS):
            gs = slice(g * HD, (g + 1) * HD)
            kc = kc_ref[0, :, gs]
            ncr = kc.shape[0]
            cidx = lax.broadcasted_iota(jnp.int32, (1, ncr), 1)
            cok = (cidx >= 1) & (CMP_STRIDE * cidx + (CMP_BLOCK - CMP_STRIDE - 1) <= qpos4)
            p = _softmax_rows(_nt(qs[g], kc), cok)
            ocmp_ref[g] = jnp.dot(p.astype(BF16), vc_ref[0, :, gs], preferred_element_type=F32)
            psum = p[0:T]
            for r in range(1, GROUP):
                psum = psum + p[r * T:(r + 1) * T]
            imp = jnp.zeros((T, nblk), F32)
            for part in _split3(psum):
                imp = imp + jnp.dot(part, A_ref[...], preferred_element_type=F32)
            score = jnp.where(blk <= cur4, jnp.where(forced, BIG, _tile_rows(imp)), NEG)
            sel_ref[g] = _topk_mask(score, blk, N_SELECT)
        m_ref[...] = jnp.full(m_ref.shape, NEG, F32)
        l_ref[...] = jnp.zeros_like(l_ref)
        acc_ref[...] = jnp.zeros_like(acc_ref)

    def flash_update(g, s, ok, v):
        s = jnp.where(ok, s, 2.0 * NEG)
        m_i = m_ref[g]
        m_new = jnp.maximum(m_i, jnp.max(s, axis=-1, keepdims=True))
        alpha = jnp.exp(m_i - m_new)
        pe = jnp.exp(s - m_new)
        l_ref[g] = alpha * l_ref[g] + jnp.sum(pe, axis=-1, keepdims=True)
        acc_ref[g] = alpha * acc_ref[g] + jnp.dot(pe.astype(BF16), v, preferred_element_type=F32)
        m_ref[g] = m_new

    bpt = tk // SEL_BLOCK
    tiles_per_vreg = LANES // bpt
    nblk = A_ref.shape[1]
    K = jnp.concatenate([p[0] for p in kpages], axis=0).astype(BF16)
    V = jnp.concatenate([p[0] for p in vpages], axis=0).astype(BF16)
    kpos = kt * tk + lax.broadcasted_iota(jnp.int32, (1, tk), 1)
    shift = (LANES - bpt * (kt % tiles_per_vreg)) % LANES
    for g in range(KV_HEADS):
        gs = slice(g * HD, (g + 1) * HD)
        sel = sel_ref[g]
        half = sel[:, 0:LANES]
        for c in range(1, nblk // LANES):
            half = jnp.where(kt // tiles_per_vreg == c, sel[:, c * LANES:(c + 1) * LANES], half)
        moved = pltpu.roll(half, shift, axis=1)
        selx = jnp.dot(moved.astype(BF16), E_ref[...], preferred_element_type=F32)
        ok = (selx > 0.5) & (kpos <= qpos4)
        flash_update(g, _nt(qs[g], K[:, gs]), ok, V[:, gs])

    @pl.when(kt == nkt - 1)
    def _():
        n_past = past // SEL_BLOCK
        lane = lax.broadcasted_iota(jnp.int32, (1, LANES), 1)
        gt = jax.nn.sigmoid(z_gate[...])
        kw_new = z_kw[...]
        vw_new = z_vw[...]
        kwo_ref[0, 0:wb - T, :] = kwin_ref[0, T:wb, :]
        kwo_ref[0, wb - T:wb, :] = kw_new
        vwo_ref[0, 0:wb - T, :] = vwin_ref[0, T:wb, :]
        vwo_ref[0, wb - T:wb, :] = vw_new

        def padded(rows):
            pad_ref[...] = jnp.zeros_like(pad_ref)
            pad_ref[0:T, :] = rows
            return pad_ref[...].astype(BF16)

        ks_new = padded(z_ks[...])
        vs_new = padded(z_vs[...])
        kw_pad = padded(kw_new)
        vw_pad = padded(vw_new)
        for g in range(KV_HEADS):
            gs = slice(g * HD, (g + 1) * HD)
            q = qs[g]
            sel_tail = sel_ref[g][:, n_past:n_past + 1]
            ok = (sel_tail > 0.5) & (past + lane <= qpos4) & (lane < T)
            flash_update(g, _nt(q, ks_new[:, gs]), ok, vs_new[:, gs])
            o_slc = acc_ref[g] / l_ref[g]
            widx = lax.broadcasted_iota(jnp.int32, (1, wb), 1)
            rel_c = qpos4 - (past - wb + widx)
            ok_c = (rel_c >= 0) & (rel_c < WINDOW) & (past - wb + widx >= 0)
            rel_n = qpos4 - (past + lane)
            ok_n = (rel_n >= 0) & (rel_n < WINDOW) & (lane < T)
            s_c = _nt(q, kwin_ref[0, :, gs].astype(BF16))
            s_n = _nt(q, kw_pad[:, gs])
            pw = _softmax_rows(jnp.concatenate([s_c, s_n], axis=1), jnp.concatenate([ok_c, ok_n], axis=1))
            o_win = jnp.dot(pw[:, :wb].astype(BF16), vwin_ref[0, :, gs].astype(BF16),
                            preferred_element_type=F32)
            o_win = o_win + jnp.dot(pw[:, wb:].astype(BF16), vw_pad[:, gs], preferred_element_type=F32)
            o_cmp = ocmp_ref[g]
            for r in range(GROUP):
                rs = slice(r * T, (r + 1) * T)
                c0 = (g * GROUP + r) * 3
                o = (gt[:, c0:c0 + 1] * o_cmp[rs] + gt[:, c0 + 1:c0 + 2] * o_slc[rs]
                     + gt[:, c0 + 2:c0 + 3] * o_win[rs])
                o_ref[:, (g * GROUP + r) * HD:(g * GROUP + r + 1) * HD] = o


def _nsa_sample(z32, kc, vc, pool_k, pool_v, kwin, vwin, page_table, A, E, *, B, T, G):
    NP = page_table.shape[1]
    past = NP * PAGE_SIZE
    wb = kwin.shape[1]
    ncr = kc.shape[1]
    kvw = KV_HEADS * HD
    R = GROUP * T
    nblk = A.shape[1]

    def page_spec(k):
        return pl.BlockSpec((1, PAGE_SIZE, kvw), lambda b, kt, pt: (pt[b, kt * G + k], 0, 0))

    def zcol(width, col):
        return pl.BlockSpec((T, width), lambda b, kt, pt: (b, col * LANES // width))

    const = lambda *shape: pl.BlockSpec(shape, lambda b, kt, pt: (0,) * len(shape))
    per_b = lambda *shape: pl.BlockSpec((1,) + shape, lambda b, kt, pt: (b,) + (0,) * len(shape))
    in_specs = [page_spec(k) for k in range(G)] * 2 + [
        zcol(A_HEADS * HD, COL_Q), zcol(kvw, COL_KS), zcol(kvw, COL_VS), zcol(kvw, COL_KW), zcol(kvw, COL_VW),
        zcol(LANES, COL_SMALL), per_b(ncr, kvw), per_b(ncr, kvw), per_b(wb, kvw), per_b(wb, kvw),
        const(*A.shape), const(*E.shape)]
    return pl.pallas_call(
        functools.partial(_nsa_sample_kernel, G=G, T=T, past=past, wb=wb),
        grid_spec=pltpu.PrefetchScalarGridSpec(
            num_scalar_prefetch=1,
            grid=(B, NP // G),
            in_specs=in_specs,
            out_specs=[pl.BlockSpec((T, A_HEADS * HD), lambda b, kt, pt: (b, 0)), per_b(wb, kvw), per_b(wb, kvw)],
            scratch_shapes=[
                pltpu.VMEM((KV_HEADS, R, nblk), F32), pltpu.VMEM((KV_HEADS, R, HD), F32),
                pltpu.VMEM((KV_HEADS, R, 1), F32), pltpu.VMEM((KV_HEADS, R, 1), F32),
                pltpu.VMEM((KV_HEADS, R, HD), F32), pltpu.VMEM((LANES, kvw), F32)],
        ),
        out_shape=[jax.ShapeDtypeStruct((B * T, A_HEADS * HD), F32),
                   jax.ShapeDtypeStruct((B, wb, kvw), F32), jax.ShapeDtypeStruct((B, wb, kvw), F32)],
        compiler_params=_cparams("arbitrary", "arbitrary"),
        name="nsa_sample",
    )(page_table, *([pool_k] * G), *([pool_v] * G), z32, z32, z32, z32, z32, z32, kc, vc, kwin, vwin, A, E)


def _sel_matrices(ncr, nblk_pad, tk):
    ratio = SEL_BLOCK // CMP_STRIDE
    lead = CMP_BLOCK // CMP_STRIDE - 1
    c = jnp.arange(ncr)[:, None] - 1
    j = jnp.arange(nblk_pad)[None, :]
    A = ((c >= 0) & (c >= ratio * j - lead) & (c <= ratio * j + ratio - 1)).astype(BF16)
    jj = jnp.arange(LANES)[:, None]
    cc = jnp.arange(tk)[None, :]
    E = ((jj == cc // SEL_BLOCK) & (jj < tk // SEL_BLOCK)).astype(BF16)
    return A, E


def _pack_params(p):
    d_nsa = A_HEADS * HD
    dm = M_HEADS * HD
    q_end = d_nsa
    kv_end = q_end + 6 * KV_HEADS * HD
    g_end = kv_end + 3 * A_HEADS
    qk_end = g_end + 2 * dm
    v_end = qk_end + dm
    if_end = v_end + 2 * M_HEADS
    w_in = p['w_in']
    d_model = w_in.shape[0]
    pad = N_PACKED - (COL_SMALL * LANES + 3 * A_HEADS + 2 * M_HEADS)
    w_packed = jnp.concatenate([
        w_in[:, g_end:qk_end], w_in[:, :q_end], w_in[:, qk_end:v_end], w_in[:, if_end:],
        w_in[:, q_end:kv_end], w_in[:, kv_end:g_end], w_in[:, v_end:if_end],
        jnp.zeros((d_model, pad), w_in.dtype)], axis=1).astype(BF16)
    gain = jnp.ones((N_PACKED,), F32)
    flag = jnp.zeros((N_PACKED,), F32)

    def put(vec, flg, col, g, reps):
        lo = col * LANES
        vec = vec.at[lo:lo + reps * HD].set(jnp.tile(g, reps))
        flg = flg.at[lo:lo + reps * HD].set(1.0)
        return vec, flg

    gain, flag = put(gain, flag, COL_Q, p['g_q'] * (HD ** -0.5), A_HEADS)
    gain, flag = put(gain, flag, COL_KS, p['g_ks'], KV_HEADS)
    gain, flag = put(gain, flag, COL_KW, p['g_kw'], KV_HEADS)
    gbias = jnp.zeros((LANES,), F32)
    gbias = gbias.at[IG_LANE:IG_LANE + M_HEADS].set(p['b_i'])
    gbias = gbias.at[FG_LANE:FG_LANE + M_HEADS].set(p['b_f'])
    hrow = jnp.arange(2 * M_HEADS)[:, None]
    lane = jnp.arange(LANES)[None, :]
    sel = jnp.where(hrow < M_HEADS, lane == IG_LANE + hrow, lane == FG_LANE + hrow - M_HEADS).astype(BF16)
    half = CMP_STRIDE * HD

    def w1cat(w1):
        return jnp.concatenate([w1[:half], w1[half:]], axis=1).astype(BF16)

    return dict(
        g_attn=p['g_attn'].reshape(1, -1), w_packed=w_packed, gain=gain.reshape(1, -1), flag=flag.reshape(1, -1),
        w_conv=p['w_conv'], b_conv=p['b_conv'].reshape(1, -1), gbias=gbias.reshape(1, -1), sel=sel,
        g_mnorm=p['g_mnorm'],
        w1cat_k=w1cat(p['w_phi1_k']), w1cat_v=w1cat(p['w_phi1_v']),
        cb_k=_pe_bias(p['pe_k'], p['w_phi1_k']), cb_v=_pe_bias(p['pe_v'], p['w_phi1_v']),
        w2_k=p['w_phi2_k'].astype(BF16), w2_v=p['w_phi2_v'].astype(BF16), g_kc=p['g_kc'].reshape(1, -1),
        wo_a=p['w_out'][:d_nsa].astype(BF16), wo_m=p['w_out'][d_nsa:].astype(BF16),
        g_mlp=p['g_mlp'].reshape(1, -1), w_up=p['w_up'].astype(BF16), w_down=p['w_down'].astype(BF16),
    )


def _kv_cols(z32, col, n):
    return z32[:, col * LANES:(col + n) * LANES]


def _prompt_pass(x, pk):
    T = x.shape[0]
    dm = M_HEADS * HD
    tm = min(T, 1024)
    z32, z16 = _inproj(x, pk['g_attn'], pk['w_packed'], pk['gain'], pk['flag'], tm=tm)
    kc_raw = _kv_cols(z32, COL_KC, KV_HEADS)
    vc_raw = _kv_cols(z32, COL_VC, KV_HEADS)
    ks = _kv_cols(z32, COL_KS, KV_HEADS)
    vs = _kv_cols(z32, COL_VS, KV_HEADS)
    wb = min(WINDOW, T)
    kw = _kv_cols(z32, COL_KW, KV_HEADS)[T - wb:]
    vw = _kv_cols(z32, COL_VW, KV_HEADS)[T - wb:]

    n_pages = T // PAGE_SIZE
    ident = jnp.arange(n_pages, dtype=jnp.int32).reshape(1, n_pages)
    ppp = PAGE_SIZE // CMP_STRIDE
    G = min(16, n_pages)
    kc = _compress(kc_raw.reshape(n_pages, ppp, -1), ident, pk['w1cat_k'], pk['cb_k'], pk['w2_k'], pk['g_kc'],
                   normalize=True, G=G)
    vc = _compress(vc_raw.reshape(n_pages, ppp, -1), ident, pk['w1cat_v'], pk['cb_v'], pk['w2_v'], pk['g_kc'],
                   normalize=False, G=G)
    tk = 512
    nblk_pad = -(-(T // SEL_BLOCK) // LANES) * LANES
    A, E = _sel_matrices(kc.shape[1], nblk_pad, tk)
    a_out = _nsa_prompt(z32, z16, kc, vc, A, E, T=T, tk=tk)

    L = min(T, 256)
    zeros = lambda *s: jnp.zeros(s, F32)
    m_out, conv, C, n, m = _mlstm(
        z32, z16, zeros(1, CONV_W - 1, 2 * dm), zeros(1, M_HEADS, HD, HD), zeros(1, M_HEADS, HD),
        zeros(1, M_HEADS), pk['w_conv'], pk['b_conv'], pk['gbias'], pk['sel'], pk['g_mnorm'],
        B=1, T=T, L=L, valid=L)
    x1 = _outproj(x, a_out, m_out[0], pk['wo_a'], pk['wo_m'], tm=min(T, 512))
    y = _mlp(x1, pk['g_mlp'], pk['w_up'], pk['w_down'], tm=min(T, 512))
    state = (kc_raw.reshape(1, T, KV_HEADS, HD), vc_raw.reshape(1, T, KV_HEADS, HD),
             ks.reshape(1, T, KV_HEADS, HD), vs.reshape(1, T, KV_HEADS, HD),
             kw.reshape(1, wb, KV_HEADS, HD), vw.reshape(1, wb, KV_HEADS, HD),
             conv, C, n, m.reshape(1, M_HEADS))
    return y, state


def _sample_pass(x, caches, states, page_table, pk):
    B, T, d_model = x.shape
    ck_cmp, cv_cmp, ck_slc, cv_slc, ck_win, cv_win = caches
    s_conv, s_C, s_n, s_m = states
    NP = page_table.shape[1]
    n_pool = ck_cmp.shape[0]
    kvw = KV_HEADS * HD
    wb = ck_win.shape[1]
    xf = x.reshape(B * T, d_model)
    z32, _ = _inproj(xf, pk['g_attn'], pk['w_packed'], pk['gain'], pk['flag'], tm=B * T)

    assert (NP * PAGE_SIZE + T) // CMP_STRIDE == NP * PAGE_SIZE // CMP_STRIDE
    ppp = PAGE_SIZE // CMP_STRIDE
    G = min(16, NP)
    kc = _compress(ck_cmp.reshape(n_pool, ppp, -1), page_table, pk['w1cat_k'], pk['cb_k'], pk['w2_k'],
                   pk['g_kc'], normalize=True, G=G)
    vc = _compress(cv_cmp.reshape(n_pool, ppp, -1), page_table, pk['w1cat_v'], pk['cb_v'], pk['w2_v'],
                   pk['g_kc'], normalize=False, G=G)
    n_sel = NP * PAGE_SIZE // SEL_BLOCK + -(-T // SEL_BLOCK)
    nblk_pad = -(-n_sel // LANES) * LANES
    A, E = _sel_matrices(kc.shape[1], nblk_pad, G * PAGE_SIZE)
    a_out, kw_o, vw_o = _nsa_sample(
        z32, kc, vc, ck_slc.reshape(n_pool, PAGE_SIZE, kvw), cv_slc.reshape(n_pool, PAGE_SIZE, kvw),
        ck_win.reshape(B, wb, kvw), cv_win.reshape(B, wb, kvw), page_table, A, E, B=B, T=T, G=G)

    m_out, conv, C, n, m = _mlstm(
        z32, z32, s_conv, s_C, s_n, s_m, pk['w_conv'], pk['b_conv'], pk['gbias'], pk['sel'], pk['g_mnorm'],
        B=B, T=T, L=LANES, valid=T)
    x1 = _outproj(xf, a_out, m_out.reshape(B * T, -1), pk['wo_a'], pk['wo_m'], tm=B * T)
    y = _mlp(x1, pk['g_mlp'], pk['w_up'], pk['w_down'], tm=B * T)
    rows = lambda col: _kv_cols(z32, col, KV_HEADS).reshape(B, T, KV_HEADS, HD)
    state = (rows(COL_KC), rows(COL_VC), rows(COL_KS), rows(COL_VS),
             kw_o.reshape(B, wb, KV_HEADS, HD), vw_o.reshape(B, wb, KV_HEADS, HD),
             conv, C, n, m.reshape(B, M_HEADS))
    return y.reshape(B, T, d_model), state


_PARAM_NAMES = ('g_attn', 'w_in', 'w_conv', 'b_conv', 'b_i', 'b_f', 'g_mnorm', 'g_q', 'g_ks', 'g_kw', 'g_kc',
                'w_phi1_k', 'w_phi2_k', 'pe_k', 'w_phi1_v', 'w_phi2_v', 'pe_v', 'w_out', 'g_mlp', 'w_up',
                'w_down')


def kernel(x_prompt, x_sample, cache_k_cmp, cache_v_cmp, cache_k_slc, cache_v_slc, cache_k_win, cache_v_win,
           state_conv, state_C, state_n, state_m, page_table, g_attn, w_in, w_conv, b_conv, b_i, b_f, g_mnorm,
           g_q, g_ks, g_kw, g_kc, w_phi1_k, w_phi2_k, pe_k, w_phi1_v, w_phi2_v, pe_v, w_out, g_mlp, w_up,
           w_down):
    weights = (g_attn, w_in, w_conv, b_conv, b_i, b_f, g_mnorm, g_q, g_ks, g_kw, g_kc, w_phi1_k, w_phi2_k,
               pe_k, w_phi1_v, w_phi2_v, pe_v, w_out, g_mlp, w_up, w_down)
    depth = w_in.shape[0]
    assert depth == 1 and x_prompt.shape[0] == 1
    pk = _pack_params({k: w[0] for k, w in zip(_PARAM_NAMES, weights)})
    yp, st_p = _prompt_pass(x_prompt[0], pk)
    ys, st_s = _sample_pass(
        x_sample, (cache_k_cmp[0], cache_v_cmp[0], cache_k_slc[0], cache_v_slc[0], cache_k_win[0], cache_v_win[0]),
        (state_conv[0], state_C[0], state_n[0], state_m[0]), page_table, pk)
    return (yp[None], ys) + tuple(s[None] for s in st_p) + tuple(s[None] for s in st_s)
```

```python
import functools
import math

import jax
import jax.numpy as jnp
from jax import lax
from jax.experimental import pallas as pl
from jax.experimental.pallas import tpu as pltpu

F32 = jnp.float32
BF16 = jnp.bfloat16

EPS = 1e-6
NEG = -1e30
BIG = 1e9

M_HEADS = 8
A_HEADS = 8
KV_HEADS = 2
GROUP = A_HEADS // KV_HEADS
HD = 128
CONV_W = 4
CMP_BLOCK = 32
CMP_STRIDE = 16
CMP_HIDDEN = 2 * HD
SEL_BLOCK = 64
N_SELECT = 16
WINDOW = 512
PAGE_SIZE = 128

LANES = 128
VMEM_LIMIT_BYTES = 56 * 1024 * 1024

COL_MQK = 0
COL_Q = 16
COL_MV = 24
COL_MO = 32
COL_KC = 40
COL_VC = 42
COL_KS = 44
COL_VS = 46
COL_KW = 48
COL_VW = 50
COL_SMALL = 52
N_PACKED = 56 * LANES
GATE_LANE = 0
IG_LANE = 24
FG_LANE = 32


def _cparams(*sem):
    return pltpu.CompilerParams(dimension_semantics=sem, vmem_limit_bytes=VMEM_LIMIT_BYTES)


def _nt(a, b):
    return lax.dot_general(a, b, (((1,), (1,)), ((), ())), preferred_element_type=F32)


def _split3(x):
    x1 = x.astype(BF16)
    r1 = x - x1.astype(F32)
    x2 = r1.astype(BF16)
    r2 = r1 - x2.astype(F32)
    return x1, x2, r2.astype(BF16)


def _inproj_kernel(x_ref, g_ref, w_ref, gain_ref, flag_ref, z32_ref, z16_ref, h_ref, *, norm_tiles):
    j = pl.program_id(1)

    @pl.when(j == 0)
    def _():
        x = x_ref[...]
        ms = jnp.mean(x * x, axis=-1, keepdims=True)
        h_ref[...] = (x * lax.rsqrt(ms + EPS) * g_ref[...]).astype(BF16)

    z = jnp.dot(h_ref[...], w_ref[...], preferred_element_type=F32)
    is_norm = functools.reduce(jnp.logical_or, [j == t for t in norm_tiles])

    @pl.when(is_norm)
    def _():
        tn = z.shape[1]
        for c in range(tn // LANES):
            sl = slice(c * LANES, (c + 1) * LANES)
            zc = z[:, sl]
            nrm = zc * lax.rsqrt(jnp.mean(zc * zc, axis=-1, keepdims=True) + EPS) * gain_ref[:, sl]
            out = jnp.where(flag_ref[:, sl] > 0.5, nrm, zc)
            z32_ref[:, sl] = out
            z16_ref[:, sl] = out.astype(BF16)

    @pl.when(jnp.logical_not(is_norm))
    def _():
        z32_ref[...] = z
        z16_ref[...] = z.astype(BF16)


def _inproj(x, g, w, gain, flag, *, tm, tn=512, norm_tiles=(4, 5, 11, 12)):
    M, K = x.shape
    N = w.shape[1]
    return pl.pallas_call(
        functools.partial(_inproj_kernel, norm_tiles=norm_tiles),
        grid=(M // tm, N // tn),
        in_specs=[
            pl.BlockSpec((tm, K), lambda i, j: (i, 0)),
            pl.BlockSpec((1, K), lambda i, j: (0, 0)),
            pl.BlockSpec((K, tn), lambda i, j: (0, j)),
            pl.BlockSpec((1, tn), lambda i, j: (0, j)),
            pl.BlockSpec((1, tn), lambda i, j: (0, j)),
        ],
        out_specs=[
            pl.BlockSpec((tm, tn), lambda i, j: (i, j)),
            pl.BlockSpec((tm, tn), lambda i, j: (i, j)),
        ],
        out_shape=[jax.ShapeDtypeStruct((M, N), F32), jax.ShapeDtypeStruct((M, N), BF16)],
        scratch_shapes=[pltpu.VMEM((tm, K), BF16)],
        compiler_params=_cparams("arbitrary", "arbitrary"),
        name="inproj",
    )(x, g, w, gain, flag)


def _outproj_kernel(x_ref, a_ref, m_ref, wa_ref, wm_ref, o_ref):
    acc = jnp.dot(a_ref[...].astype(BF16), wa_ref[...], preferred_element_type=F32)
    acc = acc + jnp.dot(m_ref[...].astype(BF16), wm_ref[...], preferred_element_type=F32)
    o_ref[...] = x_ref[...] + acc


def _outproj(x, a, m, wa, wm, *, tm, tn=1024):
    M, N = x.shape
    Ka, Km = a.shape[1], m.shape[1]
    return pl.pallas_call(
        _outproj_kernel,
        grid=(M // tm, N // tn),
        in_specs=[
            pl.BlockSpec((tm, tn), lambda i, j: (i, j)),
            pl.BlockSpec((tm, Ka), lambda i, j: (i, 0)),
            pl.BlockSpec((tm, Km), lambda i, j: (i, 0)),
            pl.BlockSpec((Ka, tn), lambda i, j: (0, j)),
            pl.BlockSpec((Km, tn), lambda i, j: (0, j)),
        ],
        out_specs=pl.BlockSpec((tm, tn), lambda i, j: (i, j)),
        out_shape=jax.ShapeDtypeStruct((M, N), F32),
        compiler_params=_cparams("arbitrary", "arbitrary"),
        name="outproj",
    )(x, a, m, wa, wm)


def _mlp_kernel(x_ref, g_ref, wu_ref, wd_ref, o_ref, h_ref, acc_ref):
    k = pl.program_id(1)

    @pl.when(k == 0)
    def _():
        x = x_ref[...]
        ms = jnp.mean(x * x, axis=-1, keepdims=True)
        h_ref[...] = (x * lax.rsqrt(ms + EPS) * g_ref[...]).astype(BF16)
        acc_ref[...] = jnp.zeros_like(acc_ref)

    u = jnp.dot(h_ref[...], wu_ref[...], preferred_element_type=F32)
    u = jnp.square(jnp.maximum(u, 0.0)).astype(BF16)
    acc_ref[...] += jnp.dot(u, wd_ref[...], preferred_element_type=F32)

    @pl.when(k == pl.num_programs(1) - 1)
    def _():
        o_ref[...] = x_ref[...] + acc_ref[...]


def _mlp(x, g, wu, wd, *, tm, tf=512):
    M, D = x.shape
    FF = wu.shape[1]
    return pl.pallas_call(
        _mlp_kernel,
        grid=(M // tm, FF // tf),
        in_specs=[
            pl.BlockSpec((tm, D), lambda i, k: (i, 0)),
            pl.BlockSpec((1, D), lambda i, k: (0, 0)),
            pl.BlockSpec((D, tf), lambda i, k: (0, k)),
            pl.BlockSpec((tf, D), lambda i, k: (k, 0)),
        ],
        out_specs=pl.BlockSpec((tm, D), lambda i, k: (i, 0)),
        out_shape=jax.ShapeDtypeStruct((M, D), F32),
        scratch_shapes=[pltpu.VMEM((tm, D), BF16), pltpu.VMEM((tm, D), F32)],
        compiler_params=_cparams("arbitrary", "arbitrary"),
        name="mlp",
    )(x, g, wu, wd)


def _silu(x):
    return x * jax.nn.sigmoid(x)


def _log_sigmoid(x):
    return jnp.minimum(x, 0.0) - jnp.log1p(jnp.exp(-jnp.abs(x)))


def _mlstm_kernel(mqk_ref, mv_ref, sm_ref, mo_ref, conv0_ref, C0_ref, n0_ref, m0_ref,
                  wconv_ref, bconv_ref, gbias_ref, sel_ref, gm_ref,
                  out_ref, conv_ref, C_ref, n_ref, m_ref,
                  xbuf_ref, *pad_refs, L, valid):
    c = pl.program_id(1)
    dm = M_HEADS * HD

    @pl.when(c == 0)
    def _():
        C_ref[...] = C0_ref[...]
        n_ref[...] = n0_ref[...]
        m_ref[...] = m0_ref[...]
        xbuf_ref[...] = jnp.zeros_like(xbuf_ref)
        xbuf_ref[8 - (CONV_W - 1):8, :] = conv0_ref[0]

    if valid < L:
        vbuf, sbuf, obuf = pad_refs
        vbuf[...] = jnp.zeros_like(vbuf)
        sbuf[...] = jnp.zeros_like(sbuf)
        obuf[...] = jnp.zeros_like(obuf)
        vbuf[0:valid, :] = mv_ref[0]
        sbuf[0:valid, :] = sm_ref[0]
        obuf[0:valid, :] = mo_ref[0]
        xbuf_ref[8:8 + valid, :] = mqk_ref[0]
        v_all, sm, mo = vbuf[...], sbuf[...], obuf[...]
    else:
        xbuf_ref[8:8 + L, :] = mqk_ref[0]
        v_all, sm, mo = mv_ref[0], sm_ref[0], mo_ref[0]

    w = wconv_ref[...]
    y = bconv_ref[...] + xbuf_ref[8:8 + L, :] * w[CONV_W - 1:CONV_W, :]
    for j in range(CONV_W - 1):
        y = y + xbuf_ref[5 + j:5 + j + L, :] * w[j:j + 1, :]
    qk = _silu(y)
    tail = xbuf_ref[5 + valid:8 + valid, :]
    conv_ref[0] = tail
    xbuf_ref[5:8, :] = tail

    row = lax.broadcasted_iota(jnp.int32, (L, 1), 0)
    live = row < valid
    gt = sm + gbias_ref[...]
    ig_all = jnp.where(live, gt, NEG)
    lf_all = jnp.where(live, _log_sigmoid(gt), 0.0)
    F_all = lf_all
    sh = 1
    while sh < L:
        F_all = F_all + jnp.where(row >= sh, pltpu.roll(F_all, sh, axis=0), 0.0)
        sh *= 2
    sel = sel_ref[...]
    head_row = lax.broadcasted_iota(jnp.int32, (2 * M_HEADS, 1), 0)
    rows = jnp.zeros((2 * M_HEADS, L), F32)
    for a, b in zip(_split3(ig_all), _split3(F_all)):
        rows = rows + jnp.where(head_row < M_HEADS, _nt(sel, a), _nt(sel, b))

    tt = lax.broadcasted_iota(jnp.int32, (L, L), 0)
    ss = lax.broadcasted_iota(jnp.int32, (L, L), 1)
    tri = ss <= tt
    for h in range(M_HEADS):
        hs = slice(h * HD, (h + 1) * HD)
        F_c = F_all[:, FG_LANE + h:FG_LANE + h + 1]
        ig_c = ig_all[:, IG_LANE + h:IG_LANE + h + 1]
        ig_r = rows[h:h + 1, :]
        F_r = rows[M_HEADS + h:M_HEADS + h + 1, :]
        m_h = m_ref[0, :, h:h + 1]
        D = jnp.where(tri, ig_r + F_c - F_r, NEG)
        a_c = m_h + F_c
        mt = jnp.maximum(a_c, jnp.max(D, axis=-1, keepdims=True))
        qf = qk[:, hs]
        kf = qk[:, dm + h * HD:dm + (h + 1) * HD] * (HD ** -0.5)
        qb, kb = qf.astype(BF16), kf.astype(BF16)
        vb = v_all[:, hs].astype(BF16)
        S = _nt(qb, kb) * jnp.exp(D - mt)
        inter = jnp.exp(a_c - mt)
        Ch = C_ref[0, h]
        nh = n_ref[0, h:h + 1, :]
        num = jnp.dot(S.astype(BF16), vb, preferred_element_type=F32) + inter * jnp.dot(
            qb, Ch.astype(BF16), preferred_element_type=F32)
        den = jnp.sum(S, axis=-1, keepdims=True) + inter * jnp.sum(qf * nh, axis=-1, keepdims=True)
        hh = num / jnp.maximum(jnp.abs(den), jnp.exp(-mt))
        m_new = mt[valid - 1:valid, :]
        F_last = F_c[valid - 1:valid, :]
        w_c = jnp.exp(ig_c + F_last - F_c - m_new)
        decay = jnp.exp(a_c[valid - 1:valid, :] - m_new)
        kw = kf * w_c
        C_ref[0, h] = decay * Ch + lax.dot_general(
            kw.astype(BF16), vb, (((0,), (0,)), ((), ())), preferred_element_type=F32)
        n_ref[0, h:h + 1, :] = decay * nh + jnp.sum(kw, axis=0, keepdims=True)
        m_ref[0, :, h:h + 1] = m_new
        hn = hh * lax.rsqrt(jnp.mean(hh * hh, axis=-1, keepdims=True) + EPS) * gm_ref[h:h + 1, :]
        res = (hn * jax.nn.sigmoid(mo[:, hs])).astype(out_ref.dtype)
        out_ref[0, :, hs] = res[0:valid, :] if valid < L else res


def _mlstm(z32, z16, conv0, C0, n0, m0, wconv, bconv, gbias, sel, gm, *, B, T, L, valid):
    dm = M_HEADS * HD
    rows = valid if valid < L else L
    nchunk = T // rows
    z32 = z32.reshape(B, T, N_PACKED)
    z16 = z16.reshape(B, T, N_PACKED)
    scratch = [pltpu.VMEM((L + 8, 2 * dm), F32)]
    if valid < L:
        scratch += [pltpu.VMEM((L, dm), F32), pltpu.VMEM((L, LANES), F32), pltpu.VMEM((L, dm), F32)]
    zv = z32 if valid < L else z16
    out_dtype = F32 if valid < L else BF16
    const = lambda *shape: pl.BlockSpec(shape, lambda b, c: (0,) * len(shape))
    return pl.pallas_call(
        functools.partial(_mlstm_kernel, L=L, valid=valid),
        grid=(B, nchunk),
        in_specs=[
            pl.BlockSpec((1, rows, 2 * dm), lambda b, c: (b, c, COL_MQK * LANES // (2 * dm))),
            pl.BlockSpec((1, rows, dm), lambda b, c: (b, c, COL_MV * LANES // dm)),
            pl.BlockSpec((1, rows, LANES), lambda b, c: (b, c, COL_SMALL)),
            pl.BlockSpec((1, rows, dm), lambda b, c: (b, c, COL_MO * LANES // dm)),
            pl.BlockSpec((1, CONV_W - 1, 2 * dm), lambda b, c: (b, 0, 0)),
            pl.BlockSpec((1, M_HEADS, HD, HD), lambda b, c: (b, 0, 0, 0)),
            pl.BlockSpec((1, M_HEADS, HD), lambda b, c: (b, 0, 0)),
            pl.BlockSpec((1, 1, M_HEADS), lambda b, c: (b, 0, 0)),
            const(CONV_W, 2 * dm), const(1, 2 * dm), const(1, LANES), const(2 * M_HEADS, LANES),
            const(M_HEADS, HD),
        ],
        out_specs=[
            pl.BlockSpec((1, rows, dm), lambda b, c: (b, c, 0)),
            pl.BlockSpec((1, CONV_W - 1, 2 * dm), lambda b, c: (b, 0, 0)),
            pl.BlockSpec((1, M_HEADS, HD, HD), lambda b, c: (b, 0, 0, 0)),
            pl.BlockSpec((1, M_HEADS, HD), lambda b, c: (b, 0, 0)),
            pl.BlockSpec((1, 1, M_HEADS), lambda b, c: (b, 0, 0)),
        ],
        out_shape=[
            jax.ShapeDtypeStruct((B, T, dm), out_dtype),
            jax.ShapeDtypeStruct((B, CONV_W - 1, 2 * dm), F32),
            jax.ShapeDtypeStruct((B, M_HEADS, HD, HD), F32),
            jax.ShapeDtypeStruct((B, M_HEADS, HD), F32),
            jax.ShapeDtypeStruct((B, 1, M_HEADS), F32),
        ],
        scratch_shapes=scratch,
        compiler_params=_cparams("arbitrary", "arbitrary"),
        name="mlstm",
    )(z32, zv, z32, z32, conv0, C0, n0, m0.reshape(B, 1, M_HEADS), wconv, bconv, gbias, sel, gm)


def _compress_kernel(pt_ref, *refs, G, normalize):
    pages = refs[:G]
    w1_ref, cb_ref, w2_ref, gain_ref, out_ref, carry_ref = refs[G:]
    pg = pl.program_id(1)
    R = G * (PAGE_SIZE // CMP_STRIDE)

    @pl.when(pg == 0)
    def _():
        carry_ref[...] = jnp.zeros_like(carry_ref)

    ppp = PAGE_SIZE // CMP_STRIDE
    xs = []
    for g in range(KV_HEADS):
        for p in pages:
            xs.append(jnp.concatenate(
                [p[pl.ds(KV_HEADS * j + g, ppp, stride=KV_HEADS * CMP_STRIDE), :] for j in range(CMP_STRIDE)],
                axis=1))
    X2 = jnp.concatenate(xs, axis=0).astype(BF16)
    LH = jnp.dot(X2, w1_ref[...], preferred_element_type=F32)
    row = lax.broadcasted_iota(jnp.int32, (R, 1), 0)
    for g in range(KV_HEADS):
        lo = LH[g * R:(g + 1) * R, :CMP_HIDDEN]
        hi = LH[g * R:(g + 1) * R, CMP_HIDDEN:]
        lo_prev = jnp.where(row == 0, carry_ref[g:g + 1, :], pltpu.roll(lo, 1, axis=0))
        carry_ref[g:g + 1, :] = lo[R - 1:R, :]
        hid = lo_prev + hi + cb_ref[0:1, :]
        out = jnp.dot(_silu(hid).astype(BF16), w2_ref[...], preferred_element_type=F32)
        if normalize:
            out = out * lax.rsqrt(jnp.mean(out * out, axis=-1, keepdims=True) + EPS) * gain_ref[...]
        out_ref[0, :, g * HD:(g + 1) * HD] = out.astype(out_ref.dtype)


def _compress(pool, page_table, w1cat, cbias, w2, gain, *, normalize, G=16):
    B, NP = page_table.shape
    ppp = PAGE_SIZE // CMP_STRIDE
    R = G * ppp

    def page_spec(k):
        return pl.BlockSpec((KV_HEADS * PAGE_SIZE, HD), lambda b, pg, pt: (pt[b, pg * G + k], 0))

    const = lambda *shape: pl.BlockSpec(shape, lambda b, pg, pt: (0,) * len(shape))
    return pl.pallas_call(
        functools.partial(_compress_kernel, G=G, normalize=normalize),
        grid_spec=pltpu.PrefetchScalarGridSpec(
            num_scalar_prefetch=1,
            grid=(B, NP // G),
            in_specs=[page_spec(k) for k in range(G)] + [
                const(CMP_STRIDE * HD, 2 * CMP_HIDDEN), const(8, CMP_HIDDEN), const(CMP_HIDDEN, HD),
                const(1, HD)],
            out_specs=pl.BlockSpec((1, R, KV_HEADS * HD), lambda b, pg, pt: (b, pg, 0)),
            scratch_shapes=[pltpu.VMEM((8, CMP_HIDDEN), F32)],
        ),
        out_shape=jax.ShapeDtypeStruct((B, NP * ppp, KV_HEADS * HD), BF16),
        compiler_params=_cparams("arbitrary", "arbitrary"),
        name="compress",
    )(page_table, *([pool] * G), w1cat, cbias, w2, gain)


def _pe_bias_kernel(pe_ref, w1_ref, o_ref):
    o_ref[...] = jnp.dot(pe_ref[...].astype(BF16), w1_ref[...].astype(BF16), preferred_element_type=F32)


def _pe_bias(pe, w1):
    pe8 = jnp.broadcast_to(pe.reshape(1, -1), (8, pe.size))
    return pl.pallas_call(
        _pe_bias_kernel,
        out_shape=jax.ShapeDtypeStruct((8, w1.shape[1]), F32),
        compiler_params=pltpu.CompilerParams(vmem_limit_bytes=VMEM_LIMIT_BYTES),
        name="pe_bias",
    )(pe8, w1)


def _softmax_rows(s, ok):
    sm = jnp.where(ok, s, NEG)
    mx = jnp.max(sm, axis=-1, keepdims=True)
    e = jnp.where(ok, jnp.exp2(sm - mx), 0.0)
    den = jnp.sum(e, axis=-1, keepdims=True)
    return e / jnp.where(den > 0.0, den, 1.0)


def _topk_mask(score, blk, n_pick):
    width = float(score.shape[-1])
    blkf = blk.astype(F32)
    sel = jnp.zeros(score.shape, F32)
    for _ in range(n_pick):
        mx = jnp.max(score, axis=-1, keepdims=True)
        idx = jnp.min(jnp.where(score == mx, blkf, width), axis=-1, keepdims=True)
        hit = blkf == idx
        sel = jnp.where(hit, 1.0, sel)
        score = jnp.where(hit, -jnp.inf, score)
    return sel


def _stack_heads(qf):
    return jnp.concatenate([qf[:, r * HD:(r + 1) * HD] for r in range(GROUP)], axis=0)


def _tile_rows(x):
    return jnp.concatenate([x] * GROUP, axis=0)


def _nsa_prompt_kernel(q_ref, kc_ref, vc_ref, ks_ref, vs_ref, *rest, tq, tk, nwin):
    kw_refs = rest[:nwin]
    vw_refs = rest[nwin:2 * nwin]
    gate_ref, A_ref, E_ref, o_ref = rest[2 * nwin:]
    g = pl.program_id(0)
    i = pl.program_id(1)
    q0 = i * tq
    q = _stack_heads(q_ref[...])
    qpos = q0 + lax.broadcasted_iota(jnp.int32, (tq, 1), 0)
    qpos4 = _tile_rows(qpos)

    kc = kc_ref[0]
    ncr = kc.shape[0]
    cidx = lax.broadcasted_iota(jnp.int32, (1, ncr), 1)
    cok = (cidx >= 1) & (CMP_STRIDE * cidx + (CMP_BLOCK - CMP_STRIDE - 1) <= qpos4)
    p = _softmax_rows(_nt(q, kc), cok)
    o_cmp = jnp.dot(p.astype(BF16), vc_ref[0], preferred_element_type=F32)
    psum = p[0:tq]
    for r in range(1, GROUP):
        psum = psum + p[r * tq:(r + 1) * tq]
    imp = jnp.zeros((tq, A_ref.shape[1]), F32)
    for part in _split3(psum):
        imp = imp + jnp.dot(part, A_ref[...], preferred_element_type=F32)
    nblk = A_ref.shape[1]
    blk = lax.broadcasted_iota(jnp.int32, (1, nblk), 1)
    cur = qpos // SEL_BLOCK
    forced = (blk == 0) | (blk == cur) | (blk == cur - 1)
    score = jnp.where(blk <= cur, jnp.where(forced, BIG, imp), NEG)
    s_w, ok_w = [], []
    for j in range(nwin):
        kwpos = q0 + (j - (nwin - 1)) * tq + lax.broadcasted_iota(jnp.int32, (1, tq), 1)
        rel = qpos4 - kwpos
        ok_w.append((rel >= 0) & (rel < WINDOW) & (kwpos >= 0))
        s_w.append(_nt(q, kw_refs[j][...]))
    pw = _softmax_rows(jnp.concatenate(s_w, axis=1), jnp.concatenate(ok_w, axis=1))
    o_win = jnp.zeros((GROUP * tq, HD), F32)
    for j in range(nwin):
        o_win = o_win + jnp.dot(pw[:, j * tq:(j + 1) * tq].astype(BF16), vw_refs[j][...],
                                preferred_element_type=F32)

    notsel = 1.0 - _topk_mask(score, blk, N_SELECT)

    bpt = tk // SEL_BLOCK
    tiles_per_vreg = LANES // bpt
    rows_q = GROUP * tq
    kpos_col = lax.broadcasted_iota(jnp.int32, (tk, 1), 0)
    qpos_row = q0 + lax.broadcasted_iota(jnp.int32, (1, rows_q), 1) % tq

    def body(kt, carry, causal):
        m_i, l_i, acc = carry
        start = pl.multiple_of(kt * tk, tk)
        half = notsel[:, 0:LANES]
        for c in range(1, nblk // LANES):
            half = jnp.where(kt // tiles_per_vreg == c, notsel[:, c * LANES:(c + 1) * LANES], half)
        shift = (LANES - bpt * (kt % tiles_per_vreg)) % LANES
        moved = pltpu.roll(half, shift, axis=1).astype(BF16)
        qa = jnp.concatenate([q, _tile_rows(moved)], axis=1)
        ka = jnp.concatenate([ks_ref[pl.ds(start, tk), :], E_ref[...]], axis=1)
        s = _nt(ka, qa)
        if causal:
            s = jnp.where(start + kpos_col <= qpos_row, s, 2.0 * NEG)
        m_new = jnp.maximum(m_i, jnp.max(s, axis=0, keepdims=True))
        alpha = jnp.exp2(m_i - m_new)
        pe = jnp.exp2(s - m_new)
        l_new = alpha * l_i + jnp.sum(pe, axis=0, keepdims=True)
        acc = alpha * acc + jnp.dot(vs_ref[:, pl.ds(start, tk)], pe.astype(BF16), preferred_element_type=F32)
        return m_new, l_new, acc

    nkt = (q0 + tq + tk - 1) // tk
    init = (jnp.full((1, rows_q), NEG, F32), jnp.zeros((1, rows_q), F32), jnp.zeros((HD, rows_q), F32))
    def body2(j, carry):
        return body(2 * j + 1, body(2 * j, carry, False), False)

    n_full = nkt - 1
    carry = lax.fori_loop(0, n_full // 2, body2, init)
    carry = lax.cond(n_full % 2 == 1, lambda c: body(n_full - 1, c, False), lambda c: c, carry)
    _, l_f, acc_f = body(nkt - 1, carry, True)
    o_slc = (acc_f / l_f).T

    gt = jax.nn.sigmoid(gate_ref[...])
    gt = jnp.where(g == 0, gt, pltpu.roll(gt, LANES - 3 * GROUP, axis=1))
    for r in range(GROUP):
        rs = slice(r * tq, (r + 1) * tq)
        o = (gt[:, 3 * r:3 * r + 1] * o_cmp[rs] + gt[:, 3 * r + 1:3 * r + 2] * o_slc[rs]
             + gt[:, 3 * r + 2:3 * r + 3] * o_win[rs])
        o_ref[:, r * HD:(r + 1) * HD] = o.astype(o_ref.dtype)


def _nsa_prompt(z32, z16, vsT, kc, vc, A, E, *, T, tq=128, tk=512):
    nwin = WINDOW // tq + 1
    ncr = kc.shape[1]
    gw = GROUP * HD

    def win_spec(col, j):
        return pl.BlockSpec((tq, HD), lambda g, i: (jnp.maximum(i + j - (nwin - 1), 0), col + g))

    const = lambda *shape: pl.BlockSpec(shape, lambda g, i: (0,) * len(shape))
    in_specs = [
        pl.BlockSpec((tq, gw), lambda g, i: (i, COL_Q * LANES // gw + g)),
        pl.BlockSpec((1, ncr, HD), lambda g, i: (0, 0, g)),
        pl.BlockSpec((1, ncr, HD), lambda g, i: (0, 0, g)),
        pl.BlockSpec((T, HD), lambda g, i: (0, COL_KS + g)),
        pl.BlockSpec((HD, T), lambda g, i: (g, 0)),
    ]
    in_specs += [win_spec(COL_KW, j) for j in range(nwin)]
    in_specs += [win_spec(COL_VW, j) for j in range(nwin)]
    in_specs += [pl.BlockSpec((tq, LANES), lambda g, i: (i, COL_SMALL)), const(*A.shape), const(*E.shape)]
    return pl.pallas_call(
        functools.partial(_nsa_prompt_kernel, tq=tq, tk=tk, nwin=nwin),
        grid=(KV_HEADS, T // tq),
        in_specs=in_specs,
        out_specs=pl.BlockSpec((tq, gw), lambda g, i: (i, g)),
        out_shape=jax.ShapeDtypeStruct((T, KV_HEADS * gw), BF16),
        compiler_params=_cparams("arbitrary", "arbitrary"),
        name="nsa_prompt",
    )(z16, kc, vc, z16, vsT, *([z16] * (2 * nwin)), z32, A, E)


def _nsa_sample_kernel(pt_ref, *refs, G, T, past, wb):
    kpages = refs[:G]
    vpages = refs[G:2 * G]
    (z_q, z_ks, z_vs, z_kw, z_vw, z_gate, kc_ref, vc_ref, kwin_ref, vwin_ref, A_ref, E_ref,
     o_ref, kwo_ref, vwo_ref,
     sel_ref, ocmp_ref, m_ref, l_ref, acc_ref, pad_ref) = refs[2 * G:]
    kt = pl.program_id(1)
    nkt = pl.num_programs(1)
    tk = G * PAGE_SIZE
    R = GROUP * T
    gw = GROUP * HD
    t_col = lax.broadcasted_iota(jnp.int32, (T, 1), 0)
    qpos4 = _tile_rows(past + t_col)
    qall = z_q[...].astype(BF16)
    qs = [_stack_heads(qall[:, g * gw:(g + 1) * gw]) for g in range(KV_HEADS)]

    @pl.when(kt == 0)
    def _():
        nblk = A_ref.shape[1]
        blk = lax.broadcasted_iota(jnp.int32, (1, nblk), 1)
        cur4 = qpos4 // SEL_BLOCK
        forced = (blk == 0) | (blk == cur4) | (blk == cur4 - 1)
        for g in range(KV_HEADS):
            gs = slice(g * HD, (g + 1) * HD)
            kc = kc_ref[0, :, gs]
            ncr = kc.shape[0]
            cidx = lax.broadcasted_iota(jnp.int32, (1, ncr), 1)
            cok = (cidx >= 1) & (CMP_STRIDE * cidx + (CMP_BLOCK - CMP_STRIDE - 1) <= qpos4)
            p = _softmax_rows(_nt(qs[g], kc), cok)
            ocmp_ref[g] = jnp.dot(p.astype(BF16), vc_ref[0, :, gs], preferred_element_type=F32)
            psum = p[0:T]
            for r in range(1, GROUP):
                psum = psum + p[r * T:(r + 1) * T]
            imp = jnp.zeros((T, nblk), F32)
            for part in _split3(psum):
                imp = imp + jnp.dot(part, A_ref[...], preferred_element_type=F32)
            score = jnp.where(blk <= cur4, jnp.where(forced, BIG, _tile_rows(imp)), NEG)
            sel_ref[g] = _topk_mask(score, blk, N_SELECT)
        m_ref[...] = jnp.full(m_ref.shape, NEG, F32)
        l_ref[...] = jnp.zeros_like(l_ref)
        acc_ref[...] = jnp.zeros_like(acc_ref)

    def flash_update(g, s, ok, v):
        s = jnp.where(ok, s, 2.0 * NEG)
        m_i = m_ref[g]
        m_new = jnp.maximum(m_i, jnp.max(s, axis=-1, keepdims=True))
        alpha = jnp.exp2(m_i - m_new)
        pe = jnp.exp2(s - m_new)
        l_ref[g] = alpha * l_ref[g] + jnp.sum(pe, axis=-1, keepdims=True)
        acc_ref[g] = alpha * acc_ref[g] + jnp.dot(pe.astype(BF16), v, preferred_element_type=F32)
        m_ref[g] = m_new

    bpt = tk // SEL_BLOCK
    tiles_per_vreg = LANES // bpt
    nblk = A_ref.shape[1]
    def head_rows(pages, g):
        return jnp.concatenate([p[pl.ds(g, PAGE_SIZE, stride=KV_HEADS), :] for p in pages], axis=0).astype(BF16)

    kpos = kt * tk + lax.broadcasted_iota(jnp.int32, (1, tk), 1)
    shift = (LANES - bpt * (kt % tiles_per_vreg)) % LANES
    for g in range(KV_HEADS):
        gs = slice(g * HD, (g + 1) * HD)
        sel = sel_ref[g]
        half = sel[:, 0:LANES]
        for c in range(1, nblk // LANES):
            half = jnp.where(kt // tiles_per_vreg == c, sel[:, c * LANES:(c + 1) * LANES], half)
        moved = pltpu.roll(half, shift, axis=1)
        selx = jnp.dot(moved.astype(BF16), E_ref[...], preferred_element_type=F32)
        ok = (selx > 0.5) & (kpos <= qpos4)
        flash_update(g, _nt(qs[g], head_rows(kpages, g)), ok, head_rows(vpages, g))

    @pl.when(kt == nkt - 1)
    def _():
        n_past = past // SEL_BLOCK
        lane = lax.broadcasted_iota(jnp.int32, (1, LANES), 1)
        gt = jax.nn.sigmoid(z_gate[...])
        kw_new = z_kw[...]
        vw_new = z_vw[...]
        nk = KV_HEADS
        kwo_ref[0:nk * (wb - T), :] = kwin_ref[nk * T:nk * wb, :]
        vwo_ref[0:nk * (wb - T), :] = vwin_ref[nk * T:nk * wb, :]
        for g in range(nk):
            kwo_ref[pl.ds(nk * (wb - T) + g, T, stride=nk), :] = kw_new[:, g * HD:(g + 1) * HD]
            vwo_ref[pl.ds(nk * (wb - T) + g, T, stride=nk), :] = vw_new[:, g * HD:(g + 1) * HD]

        def padded(rows):
            pad_ref[...] = jnp.zeros_like(pad_ref)
            pad_ref[0:T, :] = rows
            return pad_ref[...].astype(BF16)

        ks_new = padded(z_ks[...])
        vs_new = padded(z_vs[...])
        kw_pad = padded(kw_new)
        vw_pad = padded(vw_new)
        for g in range(KV_HEADS):
            gs = slice(g * HD, (g + 1) * HD)
            q = qs[g]
            sel_tail = sel_ref[g][:, n_past:n_past + 1]
            ok = (sel_tail > 0.5) & (past + lane <= qpos4) & (lane < T)
            flash_update(g, _nt(q, ks_new[:, gs]), ok, vs_new[:, gs])
            o_slc = acc_ref[g] / l_ref[g]
            widx = lax.broadcasted_iota(jnp.int32, (1, wb), 1)
            rel_c = qpos4 - (past - wb + widx)
            ok_c = (rel_c >= 0) & (rel_c < WINDOW) & (past - wb + widx >= 0)
            rel_n = qpos4 - (past + lane)
            ok_n = (rel_n >= 0) & (rel_n < WINDOW) & (lane < T)
            s_c = _nt(q, kwin_ref[pl.ds(g, wb, stride=KV_HEADS), :].astype(BF16))
            s_n = _nt(q, kw_pad[:, gs])
            pw = _softmax_rows(jnp.concatenate([s_c, s_n], axis=1), jnp.concatenate([ok_c, ok_n], axis=1))
            o_win = jnp.dot(pw[:, :wb].astype(BF16), vwin_ref[pl.ds(g, wb, stride=KV_HEADS), :].astype(BF16),
                            preferred_element_type=F32)
            o_win = o_win + jnp.dot(pw[:, wb:].astype(BF16), vw_pad[:, gs], preferred_element_type=F32)
            o_cmp = ocmp_ref[g]
            for r in range(GROUP):
                rs = slice(r * T, (r + 1) * T)
                c0 = (g * GROUP + r) * 3
                o = (gt[:, c0:c0 + 1] * o_cmp[rs] + gt[:, c0 + 1:c0 + 2] * o_slc[rs]
                     + gt[:, c0 + 2:c0 + 3] * o_win[rs])
                o_ref[:, (g * GROUP + r) * HD:(g * GROUP + r + 1) * HD] = o


def _nsa_sample(z32, kc, vc, pool_k, pool_v, kwin, vwin, page_table, A, E, *, B, T, G):
    NP = page_table.shape[1]
    past = NP * PAGE_SIZE
    wb = kwin.shape[0] // (B * KV_HEADS)
    ncr = kc.shape[1]
    kvw = KV_HEADS * HD
    R = GROUP * T
    nblk = A.shape[1]

    def page_spec(k):
        return pl.BlockSpec((KV_HEADS * PAGE_SIZE, HD), lambda b, kt, pt: (pt[b, kt * G + k], 0))

    win_spec = pl.BlockSpec((KV_HEADS * wb, HD), lambda b, kt, pt: (b, 0))

    def zcol(width, col):
        return pl.BlockSpec((T, width), lambda b, kt, pt: (b, col * LANES // width))

    const = lambda *shape: pl.BlockSpec(shape, lambda b, kt, pt: (0,) * len(shape))
    per_b = lambda *shape: pl.BlockSpec((1,) + shape, lambda b, kt, pt: (b,) + (0,) * len(shape))
    in_specs = [page_spec(k) for k in range(G)] * 2 + [
        zcol(A_HEADS * HD, COL_Q), zcol(kvw, COL_KS), zcol(kvw, COL_VS), zcol(kvw, COL_KW), zcol(kvw, COL_VW),
        zcol(LANES, COL_SMALL), per_b(ncr, kvw), per_b(ncr, kvw), win_spec, win_spec,
        const(*A.shape), const(*E.shape)]
    return pl.pallas_call(
        functools.partial(_nsa_sample_kernel, G=G, T=T, past=past, wb=wb),
        grid_spec=pltpu.PrefetchScalarGridSpec(
            num_scalar_prefetch=1,
            grid=(B, NP // G),
            in_specs=in_specs,
            out_specs=[pl.BlockSpec((T, A_HEADS * HD), lambda b, kt, pt: (b, 0)), win_spec, win_spec],
            scratch_shapes=[
                pltpu.VMEM((KV_HEADS, R, nblk), F32), pltpu.VMEM((KV_HEADS, R, HD), F32),
                pltpu.VMEM((KV_HEADS, R, 1), F32), pltpu.VMEM((KV_HEADS, R, 1), F32),
                pltpu.VMEM((KV_HEADS, R, HD), F32), pltpu.VMEM((LANES, kvw), F32)],
        ),
        out_shape=[jax.ShapeDtypeStruct((B * T, A_HEADS * HD), F32),
                   jax.ShapeDtypeStruct(kwin.shape, F32), jax.ShapeDtypeStruct(vwin.shape, F32)],
        compiler_params=_cparams("arbitrary", "arbitrary"),
        name="nsa_sample",
    )(page_table, *([pool_k] * G), *([pool_v] * G), z32, z32, z32, z32, z32, z32, kc, vc, kwin, vwin, A, E)


def _sel_matrices(ncr, nblk_pad, tk):
    ratio = SEL_BLOCK // CMP_STRIDE
    lead = CMP_BLOCK // CMP_STRIDE - 1
    c = jnp.arange(ncr)[:, None] - 1
    j = jnp.arange(nblk_pad)[None, :]
    A = ((c >= 0) & (c >= ratio * j - lead) & (c <= ratio * j + ratio - 1)).astype(BF16)
    jj = jnp.arange(LANES)[:, None]
    cc = jnp.arange(tk)[None, :]
    E = ((jj == cc // SEL_BLOCK) & (jj < tk // SEL_BLOCK)).astype(BF16)
    return A, E


def _pack_params(p):
    d_nsa = A_HEADS * HD
    dm = M_HEADS * HD
    q_end = d_nsa
    kv_end = q_end + 6 * KV_HEADS * HD
    g_end = kv_end + 3 * A_HEADS
    qk_end = g_end + 2 * dm
    v_end = qk_end + dm
    if_end = v_end + 2 * M_HEADS
    w_in = p['w_in']
    d_model = w_in.shape[0]
    pad = N_PACKED - (COL_SMALL * LANES + 3 * A_HEADS + 2 * M_HEADS)
    w_packed = jnp.concatenate([
        w_in[:, g_end:qk_end], w_in[:, :q_end], w_in[:, qk_end:v_end], w_in[:, if_end:],
        w_in[:, q_end:kv_end], w_in[:, kv_end:g_end], w_in[:, v_end:if_end],
        jnp.zeros((d_model, pad), w_in.dtype)], axis=1).astype(BF16)
    gain = jnp.ones((N_PACKED,), F32)
    flag = jnp.zeros((N_PACKED,), F32)

    def put(vec, flg, col, g, reps):
        lo = col * LANES
        vec = vec.at[lo:lo + reps * HD].set(jnp.tile(g, reps))
        flg = flg.at[lo:lo + reps * HD].set(1.0)
        return vec, flg

    gain, flag = put(gain, flag, COL_Q, p['g_q'] * (HD ** -0.5 * math.log2(math.e)), A_HEADS)
    gain, flag = put(gain, flag, COL_KS, p['g_ks'], KV_HEADS)
    gain, flag = put(gain, flag, COL_KW, p['g_kw'], KV_HEADS)
    gbias = jnp.zeros((LANES,), F32)
    gbias = gbias.at[IG_LANE:IG_LANE + M_HEADS].set(p['b_i'])
    gbias = gbias.at[FG_LANE:FG_LANE + M_HEADS].set(p['b_f'])
    hrow = jnp.arange(2 * M_HEADS)[:, None]
    lane = jnp.arange(LANES)[None, :]
    sel = jnp.where(hrow < M_HEADS, lane == IG_LANE + hrow, lane == FG_LANE + hrow - M_HEADS).astype(BF16)
    half = CMP_STRIDE * HD

    def w1cat(w1):
        return jnp.concatenate([w1[:half], w1[half:]], axis=1).astype(BF16)

    return dict(
        g_attn=p['g_attn'].reshape(1, -1), w_packed=w_packed, gain=gain.reshape(1, -1), flag=flag.reshape(1, -1),
        w_conv=p['w_conv'], b_conv=p['b_conv'].reshape(1, -1), gbias=gbias.reshape(1, -1), sel=sel,
        g_mnorm=p['g_mnorm'],
        w1cat_k=w1cat(p['w_phi1_k']), w1cat_v=w1cat(p['w_phi1_v']),
        cb_k=_pe_bias(p['pe_k'], p['w_phi1_k']), cb_v=_pe_bias(p['pe_v'], p['w_phi1_v']),
        w2_k=p['w_phi2_k'].astype(BF16), w2_v=p['w_phi2_v'].astype(BF16), g_kc=p['g_kc'].reshape(1, -1),
        wo_a=p['w_out'][:d_nsa].astype(BF16), wo_m=p['w_out'][d_nsa:].astype(BF16),
        g_mlp=p['g_mlp'].reshape(1, -1), w_up=p['w_up'].astype(BF16), w_down=p['w_down'].astype(BF16),
    )


def _kv_cols(z32, col, n):
    return z32[:, col * LANES:(col + n) * LANES]


def _prompt_pass(x, pk):
    T = x.shape[0]
    dm = M_HEADS * HD
    tm = min(T, 1024)
    z32, z16 = _inproj(x, pk['g_attn'], pk['w_packed'], pk['gain'], pk['flag'], tm=tm)
    kc_raw = _kv_cols(z32, COL_KC, KV_HEADS)
    vc_raw = _kv_cols(z32, COL_VC, KV_HEADS)
    ks = _kv_cols(z32, COL_KS, KV_HEADS)
    vs = _kv_cols(z32, COL_VS, KV_HEADS)
    wb = min(WINDOW, T)
    kw = _kv_cols(z32, COL_KW, KV_HEADS)[T - wb:]
    vw = _kv_cols(z32, COL_VW, KV_HEADS)[T - wb:]

    n_pages = T // PAGE_SIZE
    ident = jnp.arange(n_pages, dtype=jnp.int32).reshape(1, n_pages)
    G = min(16, n_pages)
    kc = _compress(kc_raw.reshape(-1, HD), ident, pk['w1cat_k'], pk['cb_k'], pk['w2_k'], pk['g_kc'],
                   normalize=True, G=G)
    vc = _compress(vc_raw.reshape(-1, HD), ident, pk['w1cat_v'], pk['cb_v'], pk['w2_v'], pk['g_kc'],
                   normalize=False, G=G)
    tk = 512
    nblk_pad = -(-(T // SEL_BLOCK) // LANES) * LANES
    A, E = _sel_matrices(kc.shape[1], nblk_pad, tk)
    vsT = _kv_cols(z16, COL_VS, KV_HEADS).T
    a_out = _nsa_prompt(z32, z16, vsT, kc, vc, A, (2.0 * NEG * E.astype(F32)).astype(BF16).T, T=T, tk=tk)

    L = min(T, 256)
    zeros = lambda *s: jnp.zeros(s, F32)
    m_out, conv, C, n, m = _mlstm(
        z32, z16, zeros(1, CONV_W - 1, 2 * dm), zeros(1, M_HEADS, HD, HD), zeros(1, M_HEADS, HD),
        zeros(1, M_HEADS), pk['w_conv'], pk['b_conv'], pk['gbias'], pk['sel'], pk['g_mnorm'],
        B=1, T=T, L=L, valid=L)
    x1 = _outproj(x, a_out, m_out[0], pk['wo_a'], pk['wo_m'], tm=min(T, 512))
    y = _mlp(x1, pk['g_mlp'], pk['w_up'], pk['w_down'], tm=min(T, 512))
    state = (kc_raw.reshape(1, T, KV_HEADS, HD), vc_raw.reshape(1, T, KV_HEADS, HD),
             ks.reshape(1, T, KV_HEADS, HD), vs.reshape(1, T, KV_HEADS, HD),
             kw.reshape(1, wb, KV_HEADS, HD), vw.reshape(1, wb, KV_HEADS, HD),
             conv, C, n, m.reshape(1, M_HEADS))
    return y, state


def _sample_pass(x, caches, states, page_table, pk):
    B, T, d_model = x.shape
    ck_cmp, cv_cmp, ck_slc, cv_slc, ck_win, cv_win = caches
    s_conv, s_C, s_n, s_m = states
    NP = page_table.shape[1]
    wb = ck_win.shape[1]
    xf = x.reshape(B * T, d_model)
    z32, _ = _inproj(xf, pk['g_attn'], pk['w_packed'], pk['gain'], pk['flag'], tm=B * T)

    assert (NP * PAGE_SIZE + T) // CMP_STRIDE == NP * PAGE_SIZE // CMP_STRIDE
    G = min(16, NP)
    kc = _compress(ck_cmp.reshape(-1, HD), page_table, pk['w1cat_k'], pk['cb_k'], pk['w2_k'],
                   pk['g_kc'], normalize=True, G=G)
    vc = _compress(cv_cmp.reshape(-1, HD), page_table, pk['w1cat_v'], pk['cb_v'], pk['w2_v'],
                   pk['g_kc'], normalize=False, G=G)
    n_sel = NP * PAGE_SIZE // SEL_BLOCK + -(-T // SEL_BLOCK)
    nblk_pad = -(-n_sel // LANES) * LANES
    A, E = _sel_matrices(kc.shape[1], nblk_pad, G * PAGE_SIZE)
    a_out, kw_o, vw_o = _nsa_sample(
        z32, kc, vc, ck_slc.reshape(-1, HD), cv_slc.reshape(-1, HD),
        ck_win.reshape(-1, HD), cv_win.reshape(-1, HD), page_table, A, E, B=B, T=T, G=G)

    m_out, conv, C, n, m = _mlstm(
        z32, z32, s_conv, s_C, s_n, s_m, pk['w_conv'], pk['b_conv'], pk['gbias'], pk['sel'], pk['g_mnorm'],
        B=B, T=T, L=LANES, valid=T)
    x1 = _outproj(xf, a_out, m_out.reshape(B * T, -1), pk['wo_a'], pk['wo_m'], tm=B * T)
    y = _mlp(x1, pk['g_mlp'], pk['w_up'], pk['w_down'], tm=B * T)
    rows = lambda col: _kv_cols(z32, col, KV_HEADS).reshape(B, T, KV_HEADS, HD)
    state = (rows(COL_KC), rows(COL_VC), rows(COL_KS), rows(COL_VS),
             kw_o.reshape(B, wb, KV_HEADS, HD), vw_o.reshape(B, wb, KV_HEADS, HD),
             conv, C, n, m.reshape(B, M_HEADS))
    return y.reshape(B, T, d_model), state


_PARAM_NAMES = ('g_attn', 'w_in', 'w_conv', 'b_conv', 'b_i', 'b_f', 'g_mnorm', 'g_q', 'g_ks', 'g_kw', 'g_kc',
                'w_phi1_k', 'w_phi2_k', 'pe_k', 'w_phi1_v', 'w_phi2_v', 'pe_v', 'w_out', 'g_mlp', 'w_up',
                'w_down')


def kernel(x_prompt, x_sample, cache_k_cmp, cache_v_cmp, cache_k_slc, cache_v_slc, cache_k_win, cache_v_win,
           state_conv, state_C, state_n, state_m, page_table, g_attn, w_in, w_conv, b_conv, b_i, b_f, g_mnorm,
           g_q, g_ks, g_kw, g_kc, w_phi1_k, w_phi2_k, pe_k, w_phi1_v, w_phi2_v, pe_v, w_out, g_mlp, w_up,
           w_down):
    weights = (g_attn, w_in, w_conv, b_conv, b_i, b_f, g_mnorm, g_q, g_ks, g_kw, g_kc, w_phi1_k, w_phi2_k,
               pe_k, w_phi1_v, w_phi2_v, pe_v, w_out, g_mlp, w_up, w_down)
    depth = w_in.shape[0]
    assert depth == 1 and x_prompt.shape[0] == 1
    pk = _pack_params({k: w[0] for k, w in zip(_PARAM_NAMES, weights)})
    yp, st_p = _prompt_pass(x_prompt[0], pk)
    ys, st_s = _sample_pass(
        x_sample, (cache_k_cmp[0], cache_v_cmp[0], cache_k_slc[0], cache_v_slc[0], cache_k_win[0], cache_v_win[0]),
        (state_conv[0], state_C[0], state_n[0], state_m[0]), page_table, pk)
    return (yp[None], ys) + tuple(s[None] for s in st_p) + tuple(s[None] for s in st_s)
```

```python
import functools
import math

import jax
import jax.numpy as jnp
from jax import lax
from jax.experimental import pallas as pl
from jax.experimental.pallas import tpu as pltpu

F32 = jnp.float32
BF16 = jnp.bfloat16

EPS = 1e-6
NEG = -1e30
BIG = 1e9

M_HEADS = 8
A_HEADS = 8
KV_HEADS = 2
GROUP = A_HEADS // KV_HEADS
HD = 128
CONV_W = 4
CMP_BLOCK = 32
CMP_STRIDE = 16
CMP_HIDDEN = 2 * HD
SEL_BLOCK = 64
N_SELECT = 16
WINDOW = 512
PAGE_SIZE = 128

LANES = 128
VMEM_LIMIT_BYTES = 56 * 1024 * 1024

COL_MQK = 0
COL_Q = 16
COL_MV = 24
COL_MO = 32
COL_KC = 40
COL_VC = 42
COL_KS = 44
COL_VS = 46
COL_KW = 48
COL_VW = 50
COL_SMALL = 52
N_PACKED = 56 * LANES
GATE_LANE = 0
IG_LANE = 24
FG_LANE = 32


def _cparams(*sem):
    return pltpu.CompilerParams(dimension_semantics=sem, vmem_limit_bytes=VMEM_LIMIT_BYTES)


def _nt(a, b):
    return lax.dot_general(a, b, (((1,), (1,)), ((), ())), preferred_element_type=F32)


def _split3(x):
    x1 = x.astype(BF16)
    r1 = x - x1.astype(F32)
    x2 = r1.astype(BF16)
    r2 = r1 - x2.astype(F32)
    return x1, x2, r2.astype(BF16)


def _inproj_kernel(x_ref, g_ref, w_ref, gain_ref, flag_ref, z32_ref, z16_ref, h_ref, *, norm_tiles):
    j = pl.program_id(1)

    @pl.when(j == 0)
    def _():
        x = x_ref[...]
        ms = jnp.mean(x * x, axis=-1, keepdims=True)
        h_ref[...] = (x * lax.rsqrt(ms + EPS) * g_ref[...]).astype(BF16)

    z = jnp.dot(h_ref[...], w_ref[...], preferred_element_type=F32)
    is_norm = functools.reduce(jnp.logical_or, [j == t for t in norm_tiles])

    @pl.when(is_norm)
    def _():
        tn = z.shape[1]
        for c in range(tn // LANES):
            sl = slice(c * LANES, (c + 1) * LANES)
            zc = z[:, sl]
            nrm = zc * lax.rsqrt(jnp.mean(zc * zc, axis=-1, keepdims=True) + EPS) * gain_ref[:, sl]
            out = jnp.where(flag_ref[:, sl] > 0.5, nrm, zc)
            z32_ref[:, sl] = out
            z16_ref[:, sl] = out.astype(BF16)

    @pl.when(jnp.logical_not(is_norm))
    def _():
        z32_ref[...] = z
        z16_ref[...] = z.astype(BF16)


def _inproj(x, g, w, gain, flag, *, tm, tn=512, norm_tiles=(4, 5, 11, 12)):
    M, K = x.shape
    N = w.shape[1]
    return pl.pallas_call(
        functools.partial(_inproj_kernel, norm_tiles=norm_tiles),
        grid=(M // tm, N // tn),
        in_specs=[
            pl.BlockSpec((tm, K), lambda i, j: (i, 0)),
            pl.BlockSpec((1, K), lambda i, j: (0, 0)),
            pl.BlockSpec((K, tn), lambda i, j: (0, j)),
            pl.BlockSpec((1, tn), lambda i, j: (0, j)),
            pl.BlockSpec((1, tn), lambda i, j: (0, j)),
        ],
        out_specs=[
            pl.BlockSpec((tm, tn), lambda i, j: (i, j)),
            pl.BlockSpec((tm, tn), lambda i, j: (i, j)),
        ],
        out_shape=[jax.ShapeDtypeStruct((M, N), F32), jax.ShapeDtypeStruct((M, N), BF16)],
        scratch_shapes=[pltpu.VMEM((tm, K), BF16)],
        compiler_params=_cparams("arbitrary", "arbitrary"),
        name="inproj",
    )(x, g, w, gain, flag)


def _outproj_kernel(x_ref, a_ref, m_ref, wa_ref, wm_ref, o_ref):
    acc = jnp.dot(a_ref[...].astype(BF16), wa_ref[...], preferred_element_type=F32)
    acc = acc + jnp.dot(m_ref[...].astype(BF16), wm_ref[...], preferred_element_type=F32)
    o_ref[...] = x_ref[...] + acc


def _outproj(x, a, m, wa, wm, *, tm, tn=1024):
    M, N = x.shape
    Ka, Km = a.shape[1], m.shape[1]
    return pl.pallas_call(
        _outproj_kernel,
        grid=(M // tm, N // tn),
        in_specs=[
            pl.BlockSpec((tm, tn), lambda i, j: (i, j)),
            pl.BlockSpec((tm, Ka), lambda i, j: (i, 0)),
            pl.BlockSpec((tm, Km), lambda i, j: (i, 0)),
            pl.BlockSpec((Ka, tn), lambda i, j: (0, j)),
            pl.BlockSpec((Km, tn), lambda i, j: (0, j)),
        ],
        out_specs=pl.BlockSpec((tm, tn), lambda i, j: (i, j)),
        out_shape=jax.ShapeDtypeStruct((M, N), F32),
        compiler_params=_cparams("arbitrary", "arbitrary"),
        name="outproj",
    )(x, a, m, wa, wm)


def _mlp_kernel(x_ref, g_ref, wu_ref, wd_ref, o_ref, h_ref, acc_ref):
    k = pl.program_id(1)

    @pl.when(k == 0)
    def _():
        x = x_ref[...]
        ms = jnp.mean(x * x, axis=-1, keepdims=True)
        h_ref[...] = (x * lax.rsqrt(ms + EPS) * g_ref[...]).astype(BF16)
        acc_ref[...] = jnp.zeros_like(acc_ref)

    u = jnp.dot(h_ref[...], wu_ref[...], preferred_element_type=F32)
    u = jnp.square(jnp.maximum(u, 0.0)).astype(BF16)
    acc_ref[...] += jnp.dot(u, wd_ref[...], preferred_element_type=F32)

    @pl.when(k == pl.num_programs(1) - 1)
    def _():
        o_ref[...] = x_ref[...] + acc_ref[...]


def _mlp(x, g, wu, wd, *, tm, tf=512):
    M, D = x.shape
    FF = wu.shape[1]
    return pl.pallas_call(
        _mlp_kernel,
        grid=(M // tm, FF // tf),
        in_specs=[
            pl.BlockSpec((tm, D), lambda i, k: (i, 0)),
            pl.BlockSpec((1, D), lambda i, k: (0, 0)),
            pl.BlockSpec((D, tf), lambda i, k: (0, k)),
            pl.BlockSpec((tf, D), lambda i, k: (k, 0)),
        ],
        out_specs=pl.BlockSpec((tm, D), lambda i, k: (i, 0)),
        out_shape=jax.ShapeDtypeStruct((M, D), F32),
        scratch_shapes=[pltpu.VMEM((tm, D), BF16), pltpu.VMEM((tm, D), F32)],
        compiler_params=_cparams("arbitrary", "arbitrary"),
        name="mlp",
    )(x, g, wu, wd)


def _silu(x):
    return x * jax.nn.sigmoid(x)


def _log_sigmoid(x):
    return jnp.minimum(x, 0.0) - jnp.log1p(jnp.exp(-jnp.abs(x)))


def _mlstm_kernel(mqk_ref, mv_ref, sm_ref, mo_ref, conv0_ref, C0_ref, n0_ref, m0_ref,
                  wconv_ref, bconv_ref, gbias_ref, sel_ref, gm_ref,
                  out_ref, conv_ref, C_ref, n_ref, m_ref,
                  xbuf_ref, *pad_refs, L, valid):
    c = pl.program_id(1)
    dm = M_HEADS * HD

    @pl.when(c == 0)
    def _():
        C_ref[...] = C0_ref[...]
        n_ref[...] = n0_ref[...]
        m_ref[...] = m0_ref[...]
        xbuf_ref[...] = jnp.zeros_like(xbuf_ref)
        xbuf_ref[8 - (CONV_W - 1):8, :] = conv0_ref[0]

    if valid < L:
        vbuf, sbuf, obuf = pad_refs
        vbuf[...] = jnp.zeros_like(vbuf)
        sbuf[...] = jnp.zeros_like(sbuf)
        obuf[...] = jnp.zeros_like(obuf)
        vbuf[0:valid, :] = mv_ref[0]
        sbuf[0:valid, :] = sm_ref[0]
        obuf[0:valid, :] = mo_ref[0]
        xbuf_ref[8:8 + valid, :] = mqk_ref[0]
        v_all, sm, mo = vbuf[...], sbuf[...], obuf[...]
    else:
        xbuf_ref[8:8 + L, :] = mqk_ref[0]
        v_all, sm, mo = mv_ref[0], sm_ref[0], mo_ref[0]

    w = wconv_ref[...]
    y = bconv_ref[...] + xbuf_ref[8:8 + L, :] * w[CONV_W - 1:CONV_W, :]
    for j in range(CONV_W - 1):
        y = y + xbuf_ref[5 + j:5 + j + L, :] * w[j:j + 1, :]
    qk = _silu(y)
    tail = xbuf_ref[5 + valid:8 + valid, :]
    conv_ref[0] = tail
    xbuf_ref[5:8, :] = tail

    row = lax.broadcasted_iota(jnp.int32, (L, 1), 0)
    live = row < valid
    gt = sm + gbias_ref[...]
    ig_all = jnp.where(live, gt, NEG)
    lf_all = jnp.where(live, _log_sigmoid(gt), 0.0)
    F_all = lf_all
    sh = 1
    while sh < L:
        F_all = F_all + jnp.where(row >= sh, pltpu.roll(F_all, sh, axis=0), 0.0)
        sh *= 2
    sel = sel_ref[...]
    head_row = lax.broadcasted_iota(jnp.int32, (2 * M_HEADS, 1), 0)
    rows = jnp.zeros((2 * M_HEADS, L), F32)
    for a, b in zip(_split3(ig_all), _split3(F_all)):
        rows = rows + jnp.where(head_row < M_HEADS, _nt(sel, a), _nt(sel, b))

    tt = lax.broadcasted_iota(jnp.int32, (L, L), 0)
    ss = lax.broadcasted_iota(jnp.int32, (L, L), 1)
    tri = ss <= tt
    for h in range(M_HEADS):
        hs = slice(h * HD, (h + 1) * HD)
        F_c = F_all[:, FG_LANE + h:FG_LANE + h + 1]
        ig_c = ig_all[:, IG_LANE + h:IG_LANE + h + 1]
        ig_r = rows[h:h + 1, :]
        F_r = rows[M_HEADS + h:M_HEADS + h + 1, :]
        m_h = m_ref[0, :, h:h + 1]
        D = jnp.where(tri, ig_r + F_c - F_r, NEG)
        a_c = m_h + F_c
        mt = jnp.maximum(a_c, jnp.max(D, axis=-1, keepdims=True))
        qf = qk[:, hs]
        kf = qk[:, dm + h * HD:dm + (h + 1) * HD] * (HD ** -0.5)
        qb, kb = qf.astype(BF16), kf.astype(BF16)
        vb = v_all[:, hs].astype(BF16)
        S = _nt(qb, kb) * jnp.exp(D - mt)
        inter = jnp.exp(a_c - mt)
        Ch = C_ref[0, h]
        nh = n_ref[0, h:h + 1, :]
        num = jnp.dot(S.astype(BF16), vb, preferred_element_type=F32) + inter * jnp.dot(
            qb, Ch.astype(BF16), preferred_element_type=F32)
        den = jnp.sum(S, axis=-1, keepdims=True) + inter * jnp.sum(qf * nh, axis=-1, keepdims=True)
        hh = num / jnp.maximum(jnp.abs(den), jnp.exp(-mt))
        m_new = mt[valid - 1:valid, :]
        F_last = F_c[valid - 1:valid, :]
        w_c = jnp.exp(ig_c + F_last - F_c - m_new)
        decay = jnp.exp(a_c[valid - 1:valid, :] - m_new)
        kw = kf * w_c
        C_ref[0, h] = decay * Ch + lax.dot_general(
            kw.astype(BF16), vb, (((0,), (0,)), ((), ())), preferred_element_type=F32)
        n_ref[0, h:h + 1, :] = decay * nh + jnp.sum(kw, axis=0, keepdims=True)
        m_ref[0, :, h:h + 1] = m_new
        hn = hh * lax.rsqrt(jnp.mean(hh * hh, axis=-1, keepdims=True) + EPS) * gm_ref[h:h + 1, :]
        res = (hn * jax.nn.sigmoid(mo[:, hs])).astype(out_ref.dtype)
        out_ref[0, :, hs] = res[0:valid, :] if valid < L else res


def _mlstm(z32, z16, conv0, C0, n0, m0, wconv, bconv, gbias, sel, gm, *, B, T, L, valid):
    dm = M_HEADS * HD
    rows = valid if valid < L else L
    nchunk = T // rows
    z32 = z32.reshape(B, T, N_PACKED)
    z16 = z16.reshape(B, T, N_PACKED)
    scratch = [pltpu.VMEM((L + 8, 2 * dm), F32)]
    if valid < L:
        scratch += [pltpu.VMEM((L, dm), F32), pltpu.VMEM((L, LANES), F32), pltpu.VMEM((L, dm), F32)]
    zv = z32 if valid < L else z16
    out_dtype = F32 if valid < L else BF16
    const = lambda *shape: pl.BlockSpec(shape, lambda b, c: (0,) * len(shape))
    return pl.pallas_call(
        functools.partial(_mlstm_kernel, L=L, valid=valid),
        grid=(B, nchunk),
        in_specs=[
            pl.BlockSpec((1, rows, 2 * dm), lambda b, c: (b, c, COL_MQK * LANES // (2 * dm))),
            pl.BlockSpec((1, rows, dm), lambda b, c: (b, c, COL_MV * LANES // dm)),
            pl.BlockSpec((1, rows, LANES), lambda b, c: (b, c, COL_SMALL)),
            pl.BlockSpec((1, rows, dm), lambda b, c: (b, c, COL_MO * LANES // dm)),
            pl.BlockSpec((1, CONV_W - 1, 2 * dm), lambda b, c: (b, 0, 0)),
            pl.BlockSpec((1, M_HEADS, HD, HD), lambda b, c: (b, 0, 0, 0)),
            pl.BlockSpec((1, M_HEADS, HD), lambda b, c: (b, 0, 0)),
            pl.BlockSpec((1, 1, M_HEADS), lambda b, c: (b, 0, 0)),
            const(CONV_W, 2 * dm), const(1, 2 * dm), const(1, LANES), const(2 * M_HEADS, LANES),
            const(M_HEADS, HD),
        ],
        out_specs=[
            pl.BlockSpec((1, rows, dm), lambda b, c: (b, c, 0)),
            pl.BlockSpec((1, CONV_W - 1, 2 * dm), lambda b, c: (b, 0, 0)),
            pl.BlockSpec((1, M_HEADS, HD, HD), lambda b, c: (b, 0, 0, 0)),
            pl.BlockSpec((1, M_HEADS, HD), lambda b, c: (b, 0, 0)),
            pl.BlockSpec((1, 1, M_HEADS), lambda b, c: (b, 0, 0)),
        ],
        out_shape=[
            jax.ShapeDtypeStruct((B, T, dm), out_dtype),
            jax.ShapeDtypeStruct((B, CONV_W - 1, 2 * dm), F32),
            jax.ShapeDtypeStruct((B, M_HEADS, HD, HD), F32),
            jax.ShapeDtypeStruct((B, M_HEADS, HD), F32),
            jax.ShapeDtypeStruct((B, 1, M_HEADS), F32),
        ],
        scratch_shapes=scratch,
        compiler_params=_cparams("arbitrary", "arbitrary"),
        name="mlstm",
    )(z32, zv, z32, z32, conv0, C0, n0, m0.reshape(B, 1, M_HEADS), wconv, bconv, gbias, sel, gm)


def _compress_kernel(pt_ref, *refs, G, normalize):
    pages = refs[:G]
    w1_ref, cb_ref, w2_ref, gain_ref, out_ref, carry_ref = refs[G:]
    pg = pl.program_id(1)
    R = G * (PAGE_SIZE // CMP_STRIDE)

    @pl.when(pg == 0)
    def _():
        carry_ref[...] = jnp.zeros_like(carry_ref)

    ppp = PAGE_SIZE // CMP_STRIDE
    xs = []
    for g in range(KV_HEADS):
        for p in pages:
            xs.append(jnp.concatenate(
                [p[pl.ds(KV_HEADS * j + g, ppp, stride=KV_HEADS * CMP_STRIDE), :] for j in range(CMP_STRIDE)],
                axis=1))
    X2 = jnp.concatenate(xs, axis=0).astype(BF16)
    LH = jnp.dot(X2, w1_ref[...], preferred_element_type=F32)
    row = lax.broadcasted_iota(jnp.int32, (R, 1), 0)
    for g in range(KV_HEADS):
        lo = LH[g * R:(g + 1) * R, :CMP_HIDDEN]
        hi = LH[g * R:(g + 1) * R, CMP_HIDDEN:]
        lo_prev = jnp.where(row == 0, carry_ref[g:g + 1, :], pltpu.roll(lo, 1, axis=0))
        carry_ref[g:g + 1, :] = lo[R - 1:R, :]
        hid = lo_prev + hi + cb_ref[0:1, :]
        out = jnp.dot(_silu(hid).astype(BF16), w2_ref[...], preferred_element_type=F32)
        if normalize:
            out = out * lax.rsqrt(jnp.mean(out * out, axis=-1, keepdims=True) + EPS) * gain_ref[...]
        out_ref[0, :, g * HD:(g + 1) * HD] = out.astype(out_ref.dtype)


def _compress(pool, page_table, w1cat, cbias, w2, gain, *, normalize, G=16):
    B, NP = page_table.shape
    ppp = PAGE_SIZE // CMP_STRIDE
    R = G * ppp

    def page_spec(k):
        return pl.BlockSpec((KV_HEADS * PAGE_SIZE, HD), lambda b, pg, pt: (pt[b, pg * G + k], 0))

    const = lambda *shape: pl.BlockSpec(shape, lambda b, pg, pt: (0,) * len(shape))
    return pl.pallas_call(
        functools.partial(_compress_kernel, G=G, normalize=normalize),
        grid_spec=pltpu.PrefetchScalarGridSpec(
            num_scalar_prefetch=1,
            grid=(B, NP // G),
            in_specs=[page_spec(k) for k in range(G)] + [
                const(CMP_STRIDE * HD, 2 * CMP_HIDDEN), const(8, CMP_HIDDEN), const(CMP_HIDDEN, HD),
                const(1, HD)],
            out_specs=pl.BlockSpec((1, R, KV_HEADS * HD), lambda b, pg, pt: (b, pg, 0)),
            scratch_shapes=[pltpu.VMEM((8, CMP_HIDDEN), F32)],
        ),
        out_shape=jax.ShapeDtypeStruct((B, NP * ppp, KV_HEADS * HD), BF16),
        compiler_params=_cparams("arbitrary", "arbitrary"),
        name="compress",
    )(page_table, *([pool] * G), w1cat, cbias, w2, gain)


def _pe_bias_kernel(pe_ref, w1_ref, o_ref):
    o_ref[...] = jnp.dot(pe_ref[...].astype(BF16), w1_ref[...].astype(BF16), preferred_element_type=F32)


def _pe_bias(pe, w1):
    pe8 = jnp.broadcast_to(pe.reshape(1, -1), (8, pe.size))
    return pl.pallas_call(
        _pe_bias_kernel,
        out_shape=jax.ShapeDtypeStruct((8, w1.shape[1]), F32),
        compiler_params=pltpu.CompilerParams(vmem_limit_bytes=VMEM_LIMIT_BYTES),
        name="pe_bias",
    )(pe8, w1)


def _softmax_rows(s, ok):
    sm = jnp.where(ok, s, NEG)
    mx = jnp.max(sm, axis=-1, keepdims=True)
    e = jnp.where(ok, jnp.exp2(sm - mx), 0.0)
    den = jnp.sum(e, axis=-1, keepdims=True)
    return e / jnp.where(den > 0.0, den, 1.0)


def _topk_mask(score, blk, n_pick):
    width = float(score.shape[-1])
    blkf = blk.astype(F32)
    sel = jnp.zeros(score.shape, F32)
    for _ in range(n_pick):
        mx = jnp.max(score, axis=-1, keepdims=True)
        idx = jnp.min(jnp.where(score == mx, blkf, width), axis=-1, keepdims=True)
        hit = blkf == idx
        sel = jnp.where(hit, 1.0, sel)
        score = jnp.where(hit, -jnp.inf, score)
    return sel


def _stack_heads(qf):
    return jnp.concatenate([qf[:, r * HD:(r + 1) * HD] for r in range(GROUP)], axis=0)


def _tile_rows(x):
    return jnp.concatenate([x] * GROUP, axis=0)


def _nsa_prompt_kernel(q_ref, kc_ref, vc_ref, ks_ref, vs_ref, *rest, tq, tk, nwin):
    kw_refs = rest[:nwin]
    vw_refs = rest[nwin:2 * nwin]
    gate_ref, A_ref, E_ref, o_ref = rest[2 * nwin:]
    g = pl.program_id(0)
    i = pl.program_id(1)
    q0 = i * tq
    q = _stack_heads(q_ref[...])
    qpos = q0 + lax.broadcasted_iota(jnp.int32, (tq, 1), 0)
    qpos4 = _tile_rows(qpos)

    kc = kc_ref[0]
    ncr = kc.shape[0]
    cidx = lax.broadcasted_iota(jnp.int32, (1, ncr), 1)
    cok = (cidx >= 1) & (CMP_STRIDE * cidx + (CMP_BLOCK - CMP_STRIDE - 1) <= qpos4)
    p = _softmax_rows(_nt(q, kc), cok)
    o_cmp = jnp.dot(p.astype(BF16), vc_ref[0], preferred_element_type=F32)
    psum = p[0:tq]
    for r in range(1, GROUP):
        psum = psum + p[r * tq:(r + 1) * tq]
    imp = jnp.zeros((tq, A_ref.shape[1]), F32)
    for part in _split3(psum):
        imp = imp + jnp.dot(part, A_ref[...], preferred_element_type=F32)
    nblk = A_ref.shape[1]
    blk = lax.broadcasted_iota(jnp.int32, (1, nblk), 1)
    cur = qpos // SEL_BLOCK
    forced = (blk == 0) | (blk == cur) | (blk == cur - 1)
    score = jnp.where(blk <= cur, jnp.where(forced, BIG, imp), NEG)
    s_w, ok_w = [], []
    for j in range(nwin):
        kwpos = q0 + (j - (nwin - 1)) * tq + lax.broadcasted_iota(jnp.int32, (1, tq), 1)
        rel = qpos4 - kwpos
        ok_w.append((rel >= 0) & (rel < WINDOW) & (kwpos >= 0))
        s_w.append(_nt(q, kw_refs[j][...]))
    pw = _softmax_rows(jnp.concatenate(s_w, axis=1), jnp.concatenate(ok_w, axis=1))
    o_win = jnp.zeros((GROUP * tq, HD), F32)
    for j in range(nwin):
        o_win = o_win + jnp.dot(pw[:, j * tq:(j + 1) * tq].astype(BF16), vw_refs[j][...],
                                preferred_element_type=F32)

    notsel = 1.0 - _topk_mask(score, blk, N_SELECT)

    bpt = tk // SEL_BLOCK
    tiles_per_vreg = LANES // bpt
    rows_q = GROUP * tq
    kpos_col = lax.broadcasted_iota(jnp.int32, (tk, 1), 0)
    qpos_row = q0 + lax.broadcasted_iota(jnp.int32, (1, rows_q), 1) % tq

    npair = 1
    pw_ = rows_q // npair

    def body(kt, carry, causal):
        start = pl.multiple_of(kt * tk, tk)
        half = notsel[:, 0:LANES]
        for c in range(1, nblk // LANES):
            half = jnp.where(kt // tiles_per_vreg == c, notsel[:, c * LANES:(c + 1) * LANES], half)
        shift = (LANES - bpt * (kt % tiles_per_vreg)) % LANES
        moved = pltpu.roll(half, shift, axis=1).astype(BF16)
        moved2 = jnp.concatenate([moved] * (pw_ // tq), axis=0)
        ka = jnp.concatenate([ks_ref[pl.ds(start, tk), :], E_ref[...]], axis=1)
        vt = vs_ref[:, pl.ds(start, tk)]
        out = []
        for hp in range(npair):
            m_i, l_i, acc = carry[hp]
            qa = jnp.concatenate([q[hp * pw_:(hp + 1) * pw_], moved2], axis=1)
            s = _nt(ka, qa)
            if causal:
                s = jnp.where(start + kpos_col <= qpos_row[:, hp * pw_:(hp + 1) * pw_], s, 2.0 * NEG)
            m_new = jnp.maximum(m_i, jnp.max(s, axis=0, keepdims=True))
            alpha = jnp.exp2(m_i - m_new)
            pe = jnp.exp2(s - m_new)
            l_new = alpha * l_i + jnp.sum(pe, axis=0, keepdims=True)
            acc = alpha * acc + jnp.dot(vt, pe.astype(BF16), preferred_element_type=F32)
            out.append((m_new, l_new, acc))
        return tuple(out)

    nkt = (q0 + tq + tk - 1) // tk
    init = tuple((jnp.full((1, pw_), NEG, F32), jnp.zeros((1, pw_), F32), jnp.zeros((HD, pw_), F32))
                 for _ in range(npair))

    def body2(j, carry):
        return body(2 * j + 1, body(2 * j, carry, False), False)

    n_full = nkt - 1
    carry = lax.fori_loop(0, n_full // 2, body2, init)
    carry = lax.cond(n_full % 2 == 1, lambda c: body(n_full - 1, c, False), lambda c: c, carry)
    fin = body(nkt - 1, carry, True)
    o_slc = jnp.concatenate([(acc_f / l_f).T for _, l_f, acc_f in fin], axis=0)

    gt = jax.nn.sigmoid(gate_ref[...])
    gt = jnp.where(g == 0, gt, pltpu.roll(gt, LANES - 3 * GROUP, axis=1))
    for r in range(GROUP):
        rs = slice(r * tq, (r + 1) * tq)
        o = (gt[:, 3 * r:3 * r + 1] * o_cmp[rs] + gt[:, 3 * r + 1:3 * r + 2] * o_slc[rs]
             + gt[:, 3 * r + 2:3 * r + 3] * o_win[rs])
        o_ref[:, r * HD:(r + 1) * HD] = o.astype(o_ref.dtype)


def _nsa_prompt(z32, z16, vsT, kc, vc, A, E, *, T, tq=128, tk=512):
    nwin = WINDOW // tq + 1
    ncr = kc.shape[1]
    gw = GROUP * HD

    def win_spec(col, j):
        return pl.BlockSpec((tq, HD), lambda g, i: (jnp.maximum(i + j - (nwin - 1), 0), col + g))

    const = lambda *shape: pl.BlockSpec(shape, lambda g, i: (0,) * len(shape))
    in_specs = [
        pl.BlockSpec((tq, gw), lambda g, i: (i, COL_Q * LANES // gw + g)),
        pl.BlockSpec((1, ncr, HD), lambda g, i: (0, 0, g)),
        pl.BlockSpec((1, ncr, HD), lambda g, i: (0, 0, g)),
        pl.BlockSpec((T, HD), lambda g, i: (0, COL_KS + g)),
        pl.BlockSpec((HD, T), lambda g, i: (g, 0)),
    ]
    in_specs += [win_spec(COL_KW, j) for j in range(nwin)]
    in_specs += [win_spec(COL_VW, j) for j in range(nwin)]
    in_specs += [pl.BlockSpec((tq, LANES), lambda g, i: (i, COL_SMALL)), const(*A.shape), const(*E.shape)]
    return pl.pallas_call(
        functools.partial(_nsa_prompt_kernel, tq=tq, tk=tk, nwin=nwin),
        grid=(KV_HEADS, T // tq),
        in_specs=in_specs,
        out_specs=pl.BlockSpec((tq, gw), lambda g, i: (i, g)),
        out_shape=jax.ShapeDtypeStruct((T, KV_HEADS * gw), BF16),
        compiler_params=_cparams("arbitrary", "arbitrary"),
        name="nsa_prompt",
    )(z16, kc, vc, z16, vsT, *([z16] * (2 * nwin)), z32, A, E)


def _nsa_sample_kernel(pt_ref, *refs, G, T, past, wb):
    kpages = refs[:G]
    vpages = refs[G:2 * G]
    (z_q, z_ks, z_vs, z_kw, z_vw, z_gate, kc_ref, vc_ref, kwin_ref, vwin_ref, A_ref, E_ref,
     o_ref, kwo_ref, vwo_ref,
     sel_ref, ocmp_ref, m_ref, l_ref, acc_ref, pad_ref) = refs[2 * G:]
    kt = pl.program_id(1)
    nkt = pl.num_programs(1)
    tk = G * PAGE_SIZE
    R = GROUP * T
    gw = GROUP * HD
    t_col = lax.broadcasted_iota(jnp.int32, (T, 1), 0)
    qpos4 = _tile_rows(past + t_col)
    qall = z_q[...].astype(BF16)
    qs = [_stack_heads(qall[:, g * gw:(g + 1) * gw]) for g in range(KV_HEADS)]

    @pl.when(kt == 0)
    def _():
        nblk = A_ref.shape[1]
        blk = lax.broadcasted_iota(jnp.int32, (1, nblk), 1)
        cur4 = qpos4 // SEL_BLOCK
        forced = (blk == 0) | (blk == cur4) | (blk == cur4 - 1)
        for g in range(KV_HEADS):
            gs = slice(g * HD, (g + 1) * HD)
            kc = kc_ref[0, :, gs]
            ncr = kc.shape[0]
            cidx = lax.broadcasted_iota(jnp.int32, (1, ncr), 1)
            cok = (cidx >= 1) & (CMP_STRIDE * cidx + (CMP_BLOCK - CMP_STRIDE - 1) <= qpos4)
            p = _softmax_rows(_nt(qs[g], kc), cok)
            ocmp_ref[g] = jnp.dot(p.astype(BF16), vc_ref[0, :, gs], preferred_element_type=F32)
            psum = p[0:T]
            for r in range(1, GROUP):
                psum = psum + p[r * T:(r + 1) * T]
            imp = jnp.zeros((T, nblk), F32)
            for part in _split3(psum):
                imp = imp + jnp.dot(part, A_ref[...], preferred_element_type=F32)
            score = jnp.where(blk <= cur4, jnp.where(forced, BIG, _tile_rows(imp)), NEG)
            sel_ref[g] = _topk_mask(score, blk, N_SELECT)
        m_ref[...] = jnp.full(m_ref.shape, NEG, F32)
        l_ref[...] = jnp.zeros_like(l_ref)
        acc_ref[...] = jnp.zeros_like(acc_ref)

    def flash_update(g, s, ok, v):
        s = jnp.where(ok, s, 2.0 * NEG)
        m_i = m_ref[g]
        m_new = jnp.maximum(m_i, jnp.max(s, axis=-1, keepdims=True))
        alpha = jnp.exp2(m_i - m_new)
        pe = jnp.exp2(s - m_new)
        l_ref[g] = alpha * l_ref[g] + jnp.sum(pe, axis=-1, keepdims=True)
        acc_ref[g] = alpha * acc_ref[g] + jnp.dot(pe.astype(BF16), v, preferred_element_type=F32)
        m_ref[g] = m_new

    bpt = tk // SEL_BLOCK
    tiles_per_vreg = LANES // bpt
    nblk = A_ref.shape[1]
    def head_rows(pages, g):
        return jnp.concatenate([p[pl.ds(g, PAGE_SIZE, stride=KV_HEADS), :] for p in pages], axis=0).astype(BF16)

    kpos = kt * tk + lax.broadcasted_iota(jnp.int32, (1, tk), 1)
    shift = (LANES - bpt * (kt % tiles_per_vreg)) % LANES
    for g in range(KV_HEADS):
        gs = slice(g * HD, (g + 1) * HD)
        sel = sel_ref[g]
        half = sel[:, 0:LANES]
        for c in range(1, nblk // LANES):
            half = jnp.where(kt // tiles_per_vreg == c, sel[:, c * LANES:(c + 1) * LANES], half)
        moved = pltpu.roll(half, shift, axis=1)
        selx = jnp.dot(moved.astype(BF16), E_ref[...], preferred_element_type=F32)
        ok = (selx > 0.5) & (kpos <= qpos4)
        flash_update(g, _nt(qs[g], head_rows(kpages, g)), ok, head_rows(vpages, g))

    @pl.when(kt == nkt - 1)
    def _():
        n_past = past // SEL_BLOCK
        lane = lax.broadcasted_iota(jnp.int32, (1, LANES), 1)
        gt = jax.nn.sigmoid(z_gate[...])
        kw_new = z_kw[...]
        vw_new = z_vw[...]
        nk = KV_HEADS
        kwo_ref[0:nk * (wb - T), :] = kwin_ref[nk * T:nk * wb, :]
        vwo_ref[0:nk * (wb - T), :] = vwin_ref[nk * T:nk * wb, :]
        for g in range(nk):
            kwo_ref[pl.ds(nk * (wb - T) + g, T, stride=nk), :] = kw_new[:, g * HD:(g + 1) * HD]
            vwo_ref[pl.ds(nk * (wb - T) + g, T, stride=nk), :] = vw_new[:, g * HD:(g + 1) * HD]

        def padded(rows):
            pad_ref[...] = jnp.zeros_like(pad_ref)
            pad_ref[0:T, :] = rows
            return pad_ref[...].astype(BF16)

        ks_new = padded(z_ks[...])
        vs_new = padded(z_vs[...])
        kw_pad = padded(kw_new)
        vw_pad = padded(vw_new)
        for g in range(KV_HEADS):
            gs = slice(g * HD, (g + 1) * HD)
            q = qs[g]
            sel_tail = sel_ref[g][:, n_past:n_past + 1]
            ok = (sel_tail > 0.5) & (past + lane <= qpos4) & (lane < T)
            flash_update(g, _nt(q, ks_new[:, gs]), ok, vs_new[:, gs])
            o_slc = acc_ref[g] / l_ref[g]
            widx = lax.broadcasted_iota(jnp.int32, (1, wb), 1)
            rel_c = qpos4 - (past - wb + widx)
            ok_c = (rel_c >= 0) & (rel_c < WINDOW) & (past - wb + widx >= 0)
            rel_n = qpos4 - (past + lane)
            ok_n = (rel_n >= 0) & (rel_n < WINDOW) & (lane < T)
            s_c = _nt(q, kwin_ref[pl.ds(g, wb, stride=KV_HEADS), :].astype(BF16))
            s_n = _nt(q, kw_pad[:, gs])
            pw = _softmax_rows(jnp.concatenate([s_c, s_n], axis=1), jnp.concatenate([ok_c, ok_n], axis=1))
            o_win = jnp.dot(pw[:, :wb].astype(BF16), vwin_ref[pl.ds(g, wb, stride=KV_HEADS), :].astype(BF16),
                            preferred_element_type=F32)
            o_win = o_win + jnp.dot(pw[:, wb:].astype(BF16), vw_pad[:, gs], preferred_element_type=F32)
            o_cmp = ocmp_ref[g]
            for r in range(GROUP):
                rs = slice(r * T, (r + 1) * T)
                c0 = (g * GROUP + r) * 3
                o = (gt[:, c0:c0 + 1] * o_cmp[rs] + gt[:, c0 + 1:c0 + 2] * o_slc[rs]
                     + gt[:, c0 + 2:c0 + 3] * o_win[rs])
                o_ref[:, (g * GROUP + r) * HD:(g * GROUP + r + 1) * HD] = o


def _nsa_sample(z32, kc, vc, pool_k, pool_v, kwin, vwin, page_table, A, E, *, B, T, G):
    NP = page_table.shape[1]
    past = NP * PAGE_SIZE
    wb = kwin.shape[0] // (B * KV_HEADS)
    ncr = kc.shape[1]
    kvw = KV_HEADS * HD
    R = GROUP * T
    nblk = A.shape[1]

    def page_spec(k):
        return pl.BlockSpec((KV_HEADS * PAGE_SIZE, HD), lambda b, kt, pt: (pt[b, kt * G + k], 0))

    win_spec = pl.BlockSpec((KV_HEADS * wb, HD), lambda b, kt, pt: (b, 0))

    def zcol(width, col):
        return pl.BlockSpec((T, width), lambda b, kt, pt: (b, col * LANES // width))

    const = lambda *shape: pl.BlockSpec(shape, lambda b, kt, pt: (0,) * len(shape))
    per_b = lambda *shape: pl.BlockSpec((1,) + shape, lambda b, kt, pt: (b,) + (0,) * len(shape))
    in_specs = [page_spec(k) for k in range(G)] * 2 + [
        zcol(A_HEADS * HD, COL_Q), zcol(kvw, COL_KS), zcol(kvw, COL_VS), zcol(kvw, COL_KW), zcol(kvw, COL_VW),
        zcol(LANES, COL_SMALL), per_b(ncr, kvw), per_b(ncr, kvw), win_spec, win_spec,
        const(*A.shape), const(*E.shape)]
    return pl.pallas_call(
        functools.partial(_nsa_sample_kernel, G=G, T=T, past=past, wb=wb),
        grid_spec=pltpu.PrefetchScalarGridSpec(
            num_scalar_prefetch=1,
            grid=(B, NP // G),
            in_specs=in_specs,
            out_specs=[pl.BlockSpec((T, A_HEADS * HD), lambda b, kt, pt: (b, 0)), win_spec, win_spec],
            scratch_shapes=[
                pltpu.VMEM((KV_HEADS, R, nblk), F32), pltpu.VMEM((KV_HEADS, R, HD), F32),
                pltpu.VMEM((KV_HEADS, R, 1), F32), pltpu.VMEM((KV_HEADS, R, 1), F32),
                pltpu.VMEM((KV_HEADS, R, HD), F32), pltpu.VMEM((LANES, kvw), F32)],
        ),
        out_shape=[jax.ShapeDtypeStruct((B * T, A_HEADS * HD), F32),
                   jax.ShapeDtypeStruct(kwin.shape, F32), jax.ShapeDtypeStruct(vwin.shape, F32)],
        compiler_params=_cparams("arbitrary", "arbitrary"),
        name="nsa_sample",
    )(page_table, *([pool_k] * G), *([pool_v] * G), z32, z32, z32, z32, z32, z32, kc, vc, kwin, vwin, A, E)


def _sel_matrices(ncr, nblk_pad, tk):
    ratio = SEL_BLOCK // CMP_STRIDE
    lead = CMP_BLOCK // CMP_STRIDE - 1
    c = jnp.arange(ncr)[:, None] - 1
    j = jnp.arange(nblk_pad)[None, :]
    A = ((c >= 0) & (c >= ratio * j - lead) & (c <= ratio * j + ratio - 1)).astype(BF16)
    jj = jnp.arange(LANES)[:, None]
    cc = jnp.arange(tk)[None, :]
    E = ((jj == cc // SEL_BLOCK) & (jj < tk // SEL_BLOCK)).astype(BF16)
    return A, E


def _pack_params(p):
    d_nsa = A_HEADS * HD
    dm = M_HEADS * HD
    q_end = d_nsa
    kv_end = q_end + 6 * KV_HEADS * HD
    g_end = kv_end + 3 * A_HEADS
    qk_end = g_end + 2 * dm
    v_end = qk_end + dm
    if_end = v_end + 2 * M_HEADS
    w_in = p['w_in']
    d_model = w_in.shape[0]
    pad = N_PACKED - (COL_SMALL * LANES + 3 * A_HEADS + 2 * M_HEADS)
    w_packed = jnp.concatenate([
        w_in[:, g_end:qk_end], w_in[:, :q_end], w_in[:, qk_end:v_end], w_in[:, if_end:],
        w_in[:, q_end:kv_end], w_in[:, kv_end:g_end], w_in[:, v_end:if_end],
        jnp.zeros((d_model, pad), w_in.dtype)], axis=1).astype(BF16)
    gain = jnp.ones((N_PACKED,), F32)
    flag = jnp.zeros((N_PACKED,), F32)

    def put(vec, flg, col, g, reps):
        lo = col * LANES
        vec = vec.at[lo:lo + reps * HD].set(jnp.tile(g, reps))
        flg = flg.at[lo:lo + reps * HD].set(1.0)
        return vec, flg

    gain, flag = put(gain, flag, COL_Q, p['g_q'] * (HD ** -0.5 * math.log2(math.e)), A_HEADS)
    gain, flag = put(gain, flag, COL_KS, p['g_ks'], KV_HEADS)
    gain, flag = put(gain, flag, COL_KW, p['g_kw'], KV_HEADS)
    gbias = jnp.zeros((LANES,), F32)
    gbias = gbias.at[IG_LANE:IG_LANE + M_HEADS].set(p['b_i'])
    gbias = gbias.at[FG_LANE:FG_LANE + M_HEADS].set(p['b_f'])
    hrow = jnp.arange(2 * M_HEADS)[:, None]
    lane = jnp.arange(LANES)[None, :]
    sel = jnp.where(hrow < M_HEADS, lane == IG_LANE + hrow, lane == FG_LANE + hrow - M_HEADS).astype(BF16)
    half = CMP_STRIDE * HD

    def w1cat(w1):
        return jnp.concatenate([w1[:half], w1[half:]], axis=1).astype(BF16)

    return dict(
        g_attn=p['g_attn'].reshape(1, -1), w_packed=w_packed, gain=gain.reshape(1, -1), flag=flag.reshape(1, -1),
        w_conv=p['w_conv'], b_conv=p['b_conv'].reshape(1, -1), gbias=gbias.reshape(1, -1), sel=sel,
        g_mnorm=p['g_mnorm'],
        w1cat_k=w1cat(p['w_phi1_k']), w1cat_v=w1cat(p['w_phi1_v']),
        cb_k=_pe_bias(p['pe_k'], p['w_phi1_k']), cb_v=_pe_bias(p['pe_v'], p['w_phi1_v']),
        w2_k=p['w_phi2_k'].astype(BF16), w2_v=p['w_phi2_v'].astype(BF16), g_kc=p['g_kc'].reshape(1, -1),
        wo_a=p['w_out'][:d_nsa].astype(BF16), wo_m=p['w_out'][d_nsa:].astype(BF16),
        g_mlp=p['g_mlp'].reshape(1, -1), w_up=p['w_up'].astype(BF16), w_down=p['w_down'].astype(BF16),
    )


def _kv_cols(z32, col, n):
    return z32[:, col * LANES:(col + n) * LANES]


def _prompt_pass(x, pk):
    T = x.shape[0]
    dm = M_HEADS * HD
    tm = min(T, 1024)
    z32, z16 = _inproj(x, pk['g_attn'], pk['w_packed'], pk['gain'], pk['flag'], tm=tm)
    kc_raw = _kv_cols(z32, COL_KC, KV_HEADS)
    vc_raw = _kv_cols(z32, COL_VC, KV_HEADS)
    ks = _kv_cols(z32, COL_KS, KV_HEADS)
    vs = _kv_cols(z32, COL_VS, KV_HEADS)
    wb = min(WINDOW, T)
    kw = _kv_cols(z32, COL_KW, KV_HEADS)[T - wb:]
    vw = _kv_cols(z32, COL_VW, KV_HEADS)[T - wb:]

    n_pages = T // PAGE_SIZE
    ident = jnp.arange(n_pages, dtype=jnp.int32).reshape(1, n_pages)
    G = min(16, n_pages)
    kc = _compress(kc_raw.reshape(-1, HD), ident, pk['w1cat_k'], pk['cb_k'], pk['w2_k'], pk['g_kc'],
                   normalize=True, G=G)
    vc = _compress(vc_raw.reshape(-1, HD), ident, pk['w1cat_v'], pk['cb_v'], pk['w2_v'], pk['g_kc'],
                   normalize=False, G=G)
    tk = min(T, 1024)
    tq = 256
    nblk_pad = -(-(T // SEL_BLOCK) // LANES) * LANES
    A, E = _sel_matrices(kc.shape[1], nblk_pad, tk)
    vsT = _kv_cols(z16, COL_VS, KV_HEADS).T
    a_out = _nsa_prompt(z32, z16, vsT, kc, vc, A, (2.0 * NEG * E.astype(F32)).astype(BF16).T, T=T, tq=tq, tk=tk)

    L = min(T, 256)
    zeros = lambda *s: jnp.zeros(s, F32)
    m_out, conv, C, n, m = _mlstm(
        z32, z16, zeros(1, CONV_W - 1, 2 * dm), zeros(1, M_HEADS, HD, HD), zeros(1, M_HEADS, HD),
        zeros(1, M_HEADS), pk['w_conv'], pk['b_conv'], pk['gbias'], pk['sel'], pk['g_mnorm'],
        B=1, T=T, L=L, valid=L)
    x1 = _outproj(x, a_out, m_out[0], pk['wo_a'], pk['wo_m'], tm=min(T, 512))
    y = _mlp(x1, pk['g_mlp'], pk['w_up'], pk['w_down'], tm=min(T, 512))
    state = (kc_raw.reshape(1, T, KV_HEADS, HD), vc_raw.reshape(1, T, KV_HEADS, HD),
             ks.reshape(1, T, KV_HEADS, HD), vs.reshape(1, T, KV_HEADS, HD),
             kw.reshape(1, wb, KV_HEADS, HD), vw.reshape(1, wb, KV_HEADS, HD),
             conv, C, n, m.reshape(1, M_HEADS))
    return y, state


def _sample_pass(x, caches, states, page_table, pk):
    B, T, d_model = x.shape
    ck_cmp, cv_cmp, ck_slc, cv_slc, ck_win, cv_win = caches
    s_conv, s_C, s_n, s_m = states
    NP = page_table.shape[1]
    wb = ck_win.shape[1]
    xf = x.reshape(B * T, d_model)
    z32, _ = _inproj(xf, pk['g_attn'], pk['w_packed'], pk['gain'], pk['flag'], tm=B * T)

    assert (NP * PAGE_SIZE + T) // CMP_STRIDE == NP * PAGE_SIZE // CMP_STRIDE
    G = min(16, NP)
    Gc = min(32, NP)
    kc = _compress(ck_cmp.reshape(-1, HD), page_table, pk['w1cat_k'], pk['cb_k'], pk['w2_k'],
                   pk['g_kc'], normalize=True, G=Gc)
    vc = _compress(cv_cmp.reshape(-1, HD), page_table, pk['w1cat_v'], pk['cb_v'], pk['w2_v'],
                   pk['g_kc'], normalize=False, G=Gc)
    n_sel = NP * PAGE_SIZE // SEL_BLOCK + -(-T // SEL_BLOCK)
    nblk_pad = -(-n_sel // LANES) * LANES
    A, E = _sel_matrices(kc.shape[1], nblk_pad, G * PAGE_SIZE)
    a_out, kw_o, vw_o = _nsa_sample(
        z32, kc, vc, ck_slc.reshape(-1, HD), cv_slc.reshape(-1, HD),
        ck_win.reshape(-1, HD), cv_win.reshape(-1, HD), page_table, A, E, B=B, T=T, G=G)

    m_out, conv, C, n, m = _mlstm(
        z32, z32, s_conv, s_C, s_n, s_m, pk['w_conv'], pk['b_conv'], pk['gbias'], pk['sel'], pk['g_mnorm'],
        B=B, T=T, L=LANES, valid=T)
    x1 = _outproj(xf, a_out, m_out.reshape(B * T, -1), pk['wo_a'], pk['wo_m'], tm=B * T)
    y = _mlp(x1, pk['g_mlp'], pk['w_up'], pk['w_down'], tm=B * T)
    rows = lambda col: _kv_cols(z32, col, KV_HEADS).reshape(B, T, KV_HEADS, HD)
    state = (rows(COL_KC), rows(COL_VC), rows(COL_KS), rows(COL_VS),
             kw_o.reshape(B, wb, KV_HEADS, HD), vw_o.reshape(B, wb, KV_HEADS, HD),
             conv, C, n, m.reshape(B, M_HEADS))
    return y.reshape(B, T, d_model), state


_PARAM_NAMES = ('g_attn', 'w_in', 'w_conv', 'b_conv', 'b_i', 'b_f', 'g_mnorm', 'g_q', 'g_ks', 'g_kw', 'g_kc',
                'w_phi1_k', 'w_phi2_k', 'pe_k', 'w_phi1_v', 'w_phi2_v', 'pe_v', 'w_out', 'g_mlp', 'w_up',
                'w_down')


def kernel(x_prompt, x_sample, cache_k_cmp, cache_v_cmp, cache_k_slc, cache_v_slc, cache_k_win, cache_v_win,
           state_conv, state_C, state_n, state_m, page_table, g_attn, w_in, w_conv, b_conv, b_i, b_f, g_mnorm,
           g_q, g_ks, g_kw, g_kc, w_phi1_k, w_phi2_k, pe_k, w_phi1_v, w_phi2_v, pe_v, w_out, g_mlp, w_up,
           w_down):
    weights = (g_attn, w_in, w_conv, b_conv, b_i, b_f, g_mnorm, g_q, g_ks, g_kw, g_kc, w_phi1_k, w_phi2_k,
               pe_k, w_phi1_v, w_phi2_v, pe_v, w_out, g_mlp, w_up, w_down)
    depth = w_in.shape[0]
    assert depth == 1 and x_prompt.shape[0] == 1
    pk = _pack_params({k: w[0] for k, w in zip(_PARAM_NAMES, weights)})
    yp, st_p = _prompt_pass(x_prompt[0], pk)
    ys, st_s = _sample_pass(
        x_sample, (cache_k_cmp[0], cache_v_cmp[0], cache_k_slc[0], cache_v_slc[0], cache_k_win[0], cache_v_win[0]),
        (state_conv[0], state_C[0], state_n[0], state_m[0]), page_table, pk)
    return (yp[None], ys) + tuple(s[None] for s in st_p) + tuple(s[None] for s in st_s)
```

```python
import functools
import math

import jax
import jax.numpy as jnp
from jax import lax
from jax.experimental import pallas as pl
from jax.experimental.pallas import tpu as pltpu

F32 = jnp.float32
BF16 = jnp.bfloat16

EPS = 1e-6
NEG = -1e30
BIG = 1e9

M_HEADS = 8
A_HEADS = 8
KV_HEADS = 2
GROUP = A_HEADS // KV_HEADS
HD = 128
CONV_W = 4
CMP_BLOCK = 32
CMP_STRIDE = 16
CMP_HIDDEN = 2 * HD
SEL_BLOCK = 64
N_SELECT = 16
WINDOW = 512
PAGE_SIZE = 128

LANES = 128
VMEM_LIMIT_BYTES = 56 * 1024 * 1024

COL_MQK = 0
COL_Q = 16
COL_MV = 24
COL_MO = 32
COL_KC = 40
COL_VC = 42
COL_KS = 44
COL_VS = 46
COL_KW = 48
COL_VW = 50
COL_SMALL = 52
N_PACKED = 56 * LANES
GATE_LANE = 0
IG_LANE = 24
FG_LANE = 32


def _cparams(*sem):
    return pltpu.CompilerParams(dimension_semantics=sem, vmem_limit_bytes=VMEM_LIMIT_BYTES)


def _nt(a, b):
    return lax.dot_general(a, b, (((1,), (1,)), ((), ())), preferred_element_type=F32)


def _split3(x):
    x1 = x.astype(BF16)
    r1 = x - x1.astype(F32)
    x2 = r1.astype(BF16)
    r2 = r1 - x2.astype(F32)
    return x1, x2, r2.astype(BF16)


def _inproj_kernel(x_ref, g_ref, w_ref, gain_ref, flag_ref, z32_ref, z16_ref, h_ref, *, norm_tiles):
    j = pl.program_id(1)

    @pl.when(j == 0)
    def _():
        x = x_ref[...]
        ms = jnp.mean(x * x, axis=-1, keepdims=True)
        h_ref[...] = (x * lax.rsqrt(ms + EPS) * g_ref[...]).astype(BF16)

    z = jnp.dot(h_ref[...], w_ref[...], preferred_element_type=F32)
    is_norm = functools.reduce(jnp.logical_or, [j == t for t in norm_tiles])

    @pl.when(is_norm)
    def _():
        tn = z.shape[1]
        for c in range(tn // LANES):
            sl = slice(c * LANES, (c + 1) * LANES)
            zc = z[:, sl]
            nrm = zc * lax.rsqrt(jnp.mean(zc * zc, axis=-1, keepdims=True) + EPS) * gain_ref[:, sl]
            out = jnp.where(flag_ref[:, sl] > 0.5, nrm, zc)
            z32_ref[:, sl] = out
            z16_ref[:, sl] = out.astype(BF16)

    @pl.when(jnp.logical_not(is_norm))
    def _():
        z32_ref[...] = z
        z16_ref[...] = z.astype(BF16)


def _inproj(x, g, w, gain, flag, *, tm, tn=1024):
    M, K = x.shape
    N = w.shape[1]
    norm_tiles = tuple(sorted({c * LANES // tn for c in (COL_Q, COL_Q + A_HEADS - 1, COL_KS, COL_KW)}))
    return pl.pallas_call(
        functools.partial(_inproj_kernel, norm_tiles=norm_tiles),
        grid=(M // tm, N // tn),
        in_specs=[
            pl.BlockSpec((tm, K), lambda i, j: (i, 0)),
            pl.BlockSpec((1, K), lambda i, j: (0, 0)),
            pl.BlockSpec((K, tn), lambda i, j: (0, j)),
            pl.BlockSpec((1, tn), lambda i, j: (0, j)),
            pl.BlockSpec((1, tn), lambda i, j: (0, j)),
        ],
        out_specs=[
            pl.BlockSpec((tm, tn), lambda i, j: (i, j)),
            pl.BlockSpec((tm, tn), lambda i, j: (i, j)),
        ],
        out_shape=[jax.ShapeDtypeStruct((M, N), F32), jax.ShapeDtypeStruct((M, N), BF16)],
        scratch_shapes=[pltpu.VMEM((tm, K), BF16)],
        compiler_params=_cparams("arbitrary", "arbitrary"),
        name="inproj",
    )(x, g, w, gain, flag)


def _outproj_kernel(x_ref, a_ref, m_ref, wa_ref, wm_ref, o_ref):
    acc = jnp.dot(a_ref[...].astype(BF16), wa_ref[...], preferred_element_type=F32)
    acc = acc + jnp.dot(m_ref[...].astype(BF16), wm_ref[...], preferred_element_type=F32)
    o_ref[...] = x_ref[...] + acc


def _outproj(x, a, m, wa, wm, *, tm, tn=1024):
    M, N = x.shape
    Ka, Km = a.shape[1], m.shape[1]
    return pl.pallas_call(
        _outproj_kernel,
        grid=(M // tm, N // tn),
        in_specs=[
            pl.BlockSpec((tm, tn), lambda i, j: (i, j)),
            pl.BlockSpec((tm, Ka), lambda i, j: (i, 0)),
            pl.BlockSpec((tm, Km), lambda i, j: (i, 0)),
            pl.BlockSpec((Ka, tn), lambda i, j: (0, j)),
            pl.BlockSpec((Km, tn), lambda i, j: (0, j)),
        ],
        out_specs=pl.BlockSpec((tm, tn), lambda i, j: (i, j)),
        out_shape=jax.ShapeDtypeStruct((M, N), F32),
        compiler_params=_cparams("arbitrary", "arbitrary"),
        name="outproj",
    )(x, a, m, wa, wm)


def _mlp_kernel(x_ref, g_ref, wu_ref, wd_ref, o_ref, h_ref, acc_ref):
    k = pl.program_id(1)

    @pl.when(k == 0)
    def _():
        x = x_ref[...]
        ms = jnp.mean(x * x, axis=-1, keepdims=True)
        h_ref[...] = (x * lax.rsqrt(ms + EPS) * g_ref[...]).astype(BF16)
        acc_ref[...] = jnp.zeros_like(acc_ref)

    u = jnp.dot(h_ref[...], wu_ref[...], preferred_element_type=F32)
    u = jnp.square(jnp.maximum(u, 0.0)).astype(BF16)
    acc_ref[...] += jnp.dot(u, wd_ref[...], preferred_element_type=F32)

    @pl.when(k == pl.num_programs(1) - 1)
    def _():
        o_ref[...] = x_ref[...] + acc_ref[...]


def _mlp(x, g, wu, wd, *, tm, tf=1024):
    M, D = x.shape
    FF = wu.shape[1]
    return pl.pallas_call(
        _mlp_kernel,
        grid=(M // tm, FF // tf),
        in_specs=[
            pl.BlockSpec((tm, D), lambda i, k: (i, 0)),
            pl.BlockSpec((1, D), lambda i, k: (0, 0)),
            pl.BlockSpec((D, tf), lambda i, k: (0, k)),
            pl.BlockSpec((tf, D), lambda i, k: (k, 0)),
        ],
        out_specs=pl.BlockSpec((tm, D), lambda i, k: (i, 0)),
        out_shape=jax.ShapeDtypeStruct((M, D), F32),
        scratch_shapes=[pltpu.VMEM((tm, D), BF16), pltpu.VMEM((tm, D), F32)],
        compiler_params=_cparams("arbitrary", "arbitrary"),
        name="mlp",
    )(x, g, wu, wd)


def _silu(x):
    return x * jax.nn.sigmoid(x)


def _log_sigmoid(x):
    return jnp.minimum(x, 0.0) - jnp.log1p(jnp.exp(-jnp.abs(x)))


def _mlstm_kernel(mqk_ref, mv_ref, sm_ref, mo_ref, conv0_ref, C0_ref, n0_ref, m0_ref,
                  wconv_ref, bconv_ref, gbias_ref, sel_ref, gm_ref,
                  out_ref, conv_ref, C_ref, n_ref, m_ref,
                  xbuf_ref, *pad_refs, L, valid):
    c = pl.program_id(1)
    dm = M_HEADS * HD

    @pl.when(c == 0)
    def _():
        C_ref[...] = C0_ref[...]
        n_ref[...] = n0_ref[...]
        m_ref[...] = m0_ref[...]
        xbuf_ref[...] = jnp.zeros_like(xbuf_ref)
        xbuf_ref[8 - (CONV_W - 1):8, :] = conv0_ref[0]

    if valid < L:
        vbuf, sbuf, obuf = pad_refs
        vbuf[...] = jnp.zeros_like(vbuf)
        sbuf[...] = jnp.zeros_like(sbuf)
        obuf[...] = jnp.zeros_like(obuf)
        vbuf[0:valid, :] = mv_ref[0]
        sbuf[0:valid, :] = sm_ref[0]
        obuf[0:valid, :] = mo_ref[0]
        xbuf_ref[8:8 + valid, :] = mqk_ref[0]
        v_all, sm, mo = vbuf[...], sbuf[...], obuf[...]
    else:
        xbuf_ref[8:8 + L, :] = mqk_ref[0]
        v_all, sm, mo = mv_ref[0], sm_ref[0], mo_ref[0]

    w = wconv_ref[...]
    y = bconv_ref[...] + xbuf_ref[8:8 + L, :] * w[CONV_W - 1:CONV_W, :]
    for j in range(CONV_W - 1):
        y = y + xbuf_ref[5 + j:5 + j + L, :] * w[j:j + 1, :]
    qk = _silu(y)
    tail = xbuf_ref[5 + valid:8 + valid, :]
    conv_ref[0] = tail
    xbuf_ref[5:8, :] = tail

    row = lax.broadcasted_iota(jnp.int32, (L, 1), 0)
    live = row < valid
    gt = sm + gbias_ref[...]
    ig_all = jnp.where(live, gt, NEG)
    lf_all = jnp.where(live, _log_sigmoid(gt), 0.0)
    F_all = lf_all
    sh = 1
    while sh < L:
        F_all = F_all + jnp.where(row >= sh, pltpu.roll(F_all, sh, axis=0), 0.0)
        sh *= 2
    sel = sel_ref[...]
    head_row = lax.broadcasted_iota(jnp.int32, (2 * M_HEADS, 1), 0)
    rows = jnp.zeros((2 * M_HEADS, L), F32)
    for a, b in zip(_split3(ig_all), _split3(F_all)):
        rows = rows + jnp.where(head_row < M_HEADS, _nt(sel, a), _nt(sel, b))

    tt = lax.broadcasted_iota(jnp.int32, (L, L), 0)
    ss = lax.broadcasted_iota(jnp.int32, (L, L), 1)
    tri = ss <= tt
    for h in range(M_HEADS):
        hs = slice(h * HD, (h + 1) * HD)
        F_c = F_all[:, FG_LANE + h:FG_LANE + h + 1]
        ig_c = ig_all[:, IG_LANE + h:IG_LANE + h + 1]
        ig_r = rows[h:h + 1, :]
        F_r = rows[M_HEADS + h:M_HEADS + h + 1, :]
        m_h = m_ref[0, :, h:h + 1]
        D = jnp.where(tri, ig_r + F_c - F_r, NEG)
        a_c = m_h + F_c
        mt = jnp.maximum(a_c, jnp.max(D, axis=-1, keepdims=True))
        qf = qk[:, hs]
        kf = qk[:, dm + h * HD:dm + (h + 1) * HD] * (HD ** -0.5)
        qb, kb = qf.astype(BF16), kf.astype(BF16)
        vb = v_all[:, hs].astype(BF16)
        S = _nt(qb, kb) * jnp.exp(D - mt)
        inter = jnp.exp(a_c - mt)
        Ch = C_ref[0, h]
        nh = n_ref[0, h:h + 1, :]
        num = jnp.dot(S.astype(BF16), vb, preferred_element_type=F32) + inter * jnp.dot(
            qb, Ch.astype(BF16), preferred_element_type=F32)
        den = jnp.sum(S, axis=-1, keepdims=True) + inter * jnp.sum(qf * nh, axis=-1, keepdims=True)
        hh = num / jnp.maximum(jnp.abs(den), jnp.exp(-mt))
        m_new = mt[valid - 1:valid, :]
        F_last = F_c[valid - 1:valid, :]
        w_c = jnp.exp(ig_c + F_last - F_c - m_new)
        decay = jnp.exp(a_c[valid - 1:valid, :] - m_new)
        kw = kf * w_c
        C_ref[0, h] = decay * Ch + lax.dot_general(
            kw.astype(BF16), vb, (((0,), (0,)), ((), ())), preferred_element_type=F32)
        n_ref[0, h:h + 1, :] = decay * nh + jnp.sum(kw, axis=0, keepdims=True)
        m_ref[0, :, h:h + 1] = m_new
        hn = hh * lax.rsqrt(jnp.mean(hh * hh, axis=-1, keepdims=True) + EPS) * gm_ref[h:h + 1, :]
        res = (hn * jax.nn.sigmoid(mo[:, hs])).astype(out_ref.dtype)
        out_ref[0, :, hs] = res[0:valid, :] if valid < L else res


def _mlstm(z32, z16, conv0, C0, n0, m0, wconv, bconv, gbias, sel, gm, *, B, T, L, valid):
    dm = M_HEADS * HD
    rows = valid if valid < L else L
    nchunk = T // rows
    z32 = z32.reshape(B, T, N_PACKED)
    z16 = z16.reshape(B, T, N_PACKED)
    scratch = [pltpu.VMEM((L + 8, 2 * dm), F32)]
    if valid < L:
        scratch += [pltpu.VMEM((L, dm), F32), pltpu.VMEM((L, LANES), F32), pltpu.VMEM((L, dm), F32)]
    zv = z32 if valid < L else z16
    out_dtype = F32 if valid < L else BF16
    const = lambda *shape: pl.BlockSpec(shape, lambda b, c: (0,) * len(shape))
    return pl.pallas_call(
        functools.partial(_mlstm_kernel, L=L, valid=valid),
        grid=(B, nchunk),
        in_specs=[
            pl.BlockSpec((1, rows, 2 * dm), lambda b, c: (b, c, COL_MQK * LANES // (2 * dm))),
            pl.BlockSpec((1, rows, dm), lambda b, c: (b, c, COL_MV * LANES // dm)),
            pl.BlockSpec((1, rows, LANES), lambda b, c: (b, c, COL_SMALL)),
            pl.BlockSpec((1, rows, dm), lambda b, c: (b, c, COL_MO * LANES // dm)),
            pl.BlockSpec((1, CONV_W - 1, 2 * dm), lambda b, c: (b, 0, 0)),
            pl.BlockSpec((1, M_HEADS, HD, HD), lambda b, c: (b, 0, 0, 0)),
            pl.BlockSpec((1, M_HEADS, HD), lambda b, c: (b, 0, 0)),
            pl.BlockSpec((1, 1, M_HEADS), lambda b, c: (b, 0, 0)),
            const(CONV_W, 2 * dm), const(1, 2 * dm), const(1, LANES), const(2 * M_HEADS, LANES),
            const(M_HEADS, HD),
        ],
        out_specs=[
            pl.BlockSpec((1, rows, dm), lambda b, c: (b, c, 0)),
            pl.BlockSpec((1, CONV_W - 1, 2 * dm), lambda b, c: (b, 0, 0)),
            pl.BlockSpec((1, M_HEADS, HD, HD), lambda b, c: (b, 0, 0, 0)),
            pl.BlockSpec((1, M_HEADS, HD), lambda b, c: (b, 0, 0)),
            pl.BlockSpec((1, 1, M_HEADS), lambda b, c: (b, 0, 0)),
        ],
        out_shape=[
            jax.ShapeDtypeStruct((B, T, dm), out_dtype),
            jax.ShapeDtypeStruct((B, CONV_W - 1, 2 * dm), F32),
            jax.ShapeDtypeStruct((B, M_HEADS, HD, HD), F32),
            jax.ShapeDtypeStruct((B, M_HEADS, HD), F32),
            jax.ShapeDtypeStruct((B, 1, M_HEADS), F32),
        ],
        scratch_shapes=scratch,
        compiler_params=_cparams("arbitrary", "arbitrary"),
        name="mlstm",
    )(z32, zv, z32, z32, conv0, C0, n0, m0.reshape(B, 1, M_HEADS), wconv, bconv, gbias, sel, gm)


_PIECE_PITCH = KV_HEADS * CMP_STRIDE + 1


def _compress_kernel(pt_ref, *refs, G, normalize):
    pages = refs[:G]
    w1_ref, cb_ref, w2_ref, gain_ref, out_ref, carry_ref, pad_ref = refs[G:]
    pg = pl.program_id(1)
    ppp = PAGE_SIZE // CMP_STRIDE
    R = G * ppp

    @pl.when(pg == 0)
    def _():
        carry_ref[...] = jnp.zeros_like(carry_ref)

    piece = KV_HEADS * CMP_STRIDE
    for k, p in enumerate(pages):
        for q in range(ppp):
            pad_ref[pl.ds((k * ppp + q) * _PIECE_PITCH, piece), :] = p[q * piece:(q + 1) * piece, :]
    xs = []
    for g in range(KV_HEADS):
        for k in range(G):
            xs.append(jnp.concatenate(
                [pad_ref[pl.ds(k * ppp * _PIECE_PITCH + KV_HEADS * j + g, ppp, stride=_PIECE_PITCH), :]
                 for j in range(CMP_STRIDE)], axis=1))
    X2 = jnp.concatenate(xs, axis=0).astype(BF16)
    LH = jnp.dot(X2, w1_ref[...], preferred_element_type=F32)
    row = lax.broadcasted_iota(jnp.int32, (R, 1), 0)
    for g in range(KV_HEADS):
        lo = LH[g * R:(g + 1) * R, :CMP_HIDDEN]
        hi = LH[g * R:(g + 1) * R, CMP_HIDDEN:]
        lo_prev = jnp.where(row == 0, carry_ref[g:g + 1, :], pltpu.roll(lo, 1, axis=0))
        carry_ref[g:g + 1, :] = lo[R - 1:R, :]
        hid = lo_prev + hi + cb_ref[0:1, :]
        out = jnp.dot(_silu(hid).astype(BF16), w2_ref[...], preferred_element_type=F32)
        if normalize:
            out = out * lax.rsqrt(jnp.mean(out * out, axis=-1, keepdims=True) + EPS) * gain_ref[...]
        out_ref[0, :, g * HD:(g + 1) * HD] = out.astype(out_ref.dtype)


def _compress(pool, page_table, w1cat, cbias, w2, gain, *, normalize, G=16):
    B, NP = page_table.shape
    ppp = PAGE_SIZE // CMP_STRIDE
    R = G * ppp

    def page_spec(k):
        return pl.BlockSpec((KV_HEADS * PAGE_SIZE, HD), lambda b, pg, pt: (pt[b, pg * G + k], 0))

    const = lambda *shape: pl.BlockSpec(shape, lambda b, pg, pt: (0,) * len(shape))
    return pl.pallas_call(
        functools.partial(_compress_kernel, G=G, normalize=normalize),
        grid_spec=pltpu.PrefetchScalarGridSpec(
            num_scalar_prefetch=1,
            grid=(B, NP // G),
            in_specs=[page_spec(k) for k in range(G)] + [
                const(CMP_STRIDE * HD, 2 * CMP_HIDDEN), const(8, CMP_HIDDEN), const(CMP_HIDDEN, HD),
                const(1, HD)],
            out_specs=pl.BlockSpec((1, R, KV_HEADS * HD), lambda b, pg, pt: (b, pg, 0)),
            scratch_shapes=[pltpu.VMEM((8, CMP_HIDDEN), F32),
                            pltpu.VMEM((-(-G * ppp * _PIECE_PITCH // 8) * 8, HD), F32)],
        ),
        out_shape=jax.ShapeDtypeStruct((B, NP * ppp, KV_HEADS * HD), BF16),
        compiler_params=_cparams("arbitrary", "arbitrary"),
        name="compress",
    )(page_table, *([pool] * G), w1cat, cbias, w2, gain)


def _pe_bias_kernel(pe_ref, w1_ref, o_ref):
    o_ref[...] = jnp.dot(pe_ref[...].astype(BF16), w1_ref[...].astype(BF16), preferred_element_type=F32)


def _pe_bias(pe, w1):
    pe8 = jnp.broadcast_to(pe.reshape(1, -1), (8, pe.size))
    return pl.pallas_call(
        _pe_bias_kernel,
        out_shape=jax.ShapeDtypeStruct((8, w1.shape[1]), F32),
        compiler_params=pltpu.CompilerParams(vmem_limit_bytes=VMEM_LIMIT_BYTES),
        name="pe_bias",
    )(pe8, w1)


def _softmax_rows(s, ok):
    sm = jnp.where(ok, s, NEG)
    mx = jnp.max(sm, axis=-1, keepdims=True)
    e = jnp.where(ok, jnp.exp2(sm - mx), 0.0)
    den = jnp.sum(e, axis=-1, keepdims=True)
    return e / jnp.where(den > 0.0, den, 1.0)


def _topk_mask(score, blk, n_pick):
    width = float(score.shape[-1])
    blkf = blk.astype(F32)
    sel = jnp.zeros(score.shape, F32)
    for _ in range(n_pick):
        mx = jnp.max(score, axis=-1, keepdims=True)
        idx = jnp.min(jnp.where(score == mx, blkf, width), axis=-1, keepdims=True)
        hit = blkf == idx
        sel = jnp.where(hit, 1.0, sel)
        score = jnp.where(hit, -jnp.inf, score)
    return sel


def _topk_mask_cols(score_t, n_pick):
    width = float(score_t.shape[0])
    blkf = lax.broadcasted_iota(jnp.int32, (score_t.shape[0], 1), 0).astype(F32)
    sel = jnp.zeros(score_t.shape, F32)
    for _ in range(n_pick):
        mx = jnp.max(score_t, axis=0, keepdims=True)
        idx = jnp.min(jnp.where(score_t == mx, blkf, width), axis=0, keepdims=True)
        hit = blkf == idx
        sel = jnp.where(hit, 1.0, sel)
        score_t = jnp.where(hit, -jnp.inf, score_t)
    return sel


def _stack_heads(qf):
    return jnp.concatenate([qf[:, r * HD:(r + 1) * HD] for r in range(GROUP)], axis=0)


def _tile_rows(x):
    return jnp.concatenate([x] * GROUP, axis=0)


def _nsa_prompt_kernel(q_ref, kc_ref, vc_ref, ks_ref, vs_ref, *rest, tq, tk, nwin):
    kw_refs = rest[:nwin]
    vw_refs = rest[nwin:2 * nwin]
    gate_ref, A_ref, E_ref, o_ref = rest[2 * nwin:]
    g = pl.program_id(0)
    i = pl.program_id(1)
    q0 = i * tq
    q = _stack_heads(q_ref[...])
    qpos = q0 + lax.broadcasted_iota(jnp.int32, (tq, 1), 0)
    qpos4 = _tile_rows(qpos)

    kc = kc_ref[0]
    ncr = kc.shape[0]
    cidx = lax.broadcasted_iota(jnp.int32, (1, ncr), 1)
    cok = (cidx >= 1) & (CMP_STRIDE * cidx + (CMP_BLOCK - CMP_STRIDE - 1) <= qpos4)
    p = _softmax_rows(_nt(q, kc), cok)
    o_cmp = jnp.dot(p.astype(BF16), vc_ref[0], preferred_element_type=F32)
    psum = p[0:tq]
    for r in range(1, GROUP):
        psum = psum + p[r * tq:(r + 1) * tq]
    imp = jnp.zeros((tq, A_ref.shape[1]), F32)
    for part in _split3(psum):
        imp = imp + jnp.dot(part, A_ref[...], preferred_element_type=F32)
    nblk = A_ref.shape[1]
    blk = lax.broadcasted_iota(jnp.int32, (1, nblk), 1)
    cur = qpos // SEL_BLOCK
    forced = (blk == 0) | (blk == cur) | (blk == cur - 1)
    score = jnp.where(blk <= cur, jnp.where(forced, BIG, imp), NEG)
    s_w, ok_w = [], []
    for j in range(nwin):
        kwpos = q0 + (j - (nwin - 1)) * tq + lax.broadcasted_iota(jnp.int32, (1, tq), 1)
        rel = qpos4 - kwpos
        ok_w.append((rel >= 0) & (rel < WINDOW) & (kwpos >= 0))
        s_w.append(_nt(q, kw_refs[j][...]))
    pw = _softmax_rows(jnp.concatenate(s_w, axis=1), jnp.concatenate(ok_w, axis=1))
    o_win = jnp.zeros((GROUP * tq, HD), F32)
    for j in range(nwin):
        o_win = o_win + jnp.dot(pw[:, j * tq:(j + 1) * tq].astype(BF16), vw_refs[j][...],
                                preferred_element_type=F32)

    notsel = 1.0 - _topk_mask_cols(score.T, N_SELECT).T

    bpt = tk // SEL_BLOCK
    tiles_per_vreg = LANES // bpt
    rows_q = GROUP * tq
    kpos_col = lax.broadcasted_iota(jnp.int32, (tk, 1), 0)
    qpos_row = q0 + lax.broadcasted_iota(jnp.int32, (1, rows_q), 1) % tq

    npair = 1
    pw_ = rows_q // npair

    def body(kt, carry, causal):
        start = pl.multiple_of(kt * tk, tk)
        half = notsel[:, 0:LANES]
        for c in range(1, nblk // LANES):
            half = jnp.where(kt // tiles_per_vreg == c, notsel[:, c * LANES:(c + 1) * LANES], half)
        shift = (LANES - bpt * (kt % tiles_per_vreg)) % LANES
        moved = pltpu.roll(half, shift, axis=1).astype(BF16)
        moved2 = jnp.concatenate([moved] * (pw_ // tq), axis=0)
        ka = jnp.concatenate([ks_ref[pl.ds(start, tk), :], E_ref[...]], axis=1)
        vt = vs_ref[:, pl.ds(start, tk)]
        out = []
        for hp in range(npair):
            m_i, l_i, acc = carry[hp]
            qa = jnp.concatenate([q[hp * pw_:(hp + 1) * pw_], moved2], axis=1)
            s = _nt(ka, qa)
            if causal:
                s = jnp.where(start + kpos_col <= qpos_row[:, hp * pw_:(hp + 1) * pw_], s, 2.0 * NEG)
            m_new = jnp.maximum(m_i, jnp.max(s, axis=0, keepdims=True))
            alpha = jnp.exp2(m_i - m_new)
            pe = jnp.exp2(s - m_new)
            l_new = alpha * l_i + jnp.sum(pe, axis=0, keepdims=True)
            acc = alpha * acc + jnp.dot(vt, pe.astype(BF16), preferred_element_type=F32)
            out.append((m_new, l_new, acc))
        return tuple(out)

    nkt = (q0 + tq + tk - 1) // tk
    init = tuple((jnp.full((1, pw_), NEG, F32), jnp.zeros((1, pw_), F32), jnp.zeros((HD, pw_), F32))
                 for _ in range(npair))

    def body2(j, carry):
        return body(2 * j + 1, body(2 * j, carry, False), False)

    n_full = nkt - 1
    carry = lax.fori_loop(0, n_full // 2, body2, init)
    carry = lax.cond(n_full % 2 == 1, lambda c: body(n_full - 1, c, False), lambda c: c, carry)
    fin = body(nkt - 1, carry, True)
    o_slc = jnp.concatenate([(acc_f / l_f).T for _, l_f, acc_f in fin], axis=0)

    gt = jax.nn.sigmoid(gate_ref[...])
    gt = jnp.where(g == 0, gt, pltpu.roll(gt, LANES - 3 * GROUP, axis=1))
    for r in range(GROUP):
        rs = slice(r * tq, (r + 1) * tq)
        o = (gt[:, 3 * r:3 * r + 1] * o_cmp[rs] + gt[:, 3 * r + 1:3 * r + 2] * o_slc[rs]
             + gt[:, 3 * r + 2:3 * r + 3] * o_win[rs])
        o_ref[:, r * HD:(r + 1) * HD] = o.astype(o_ref.dtype)


def _nsa_prompt(z32, z16, vsT, kc, vc, A, E, *, T, tq=128, tk=512):
    nwin = WINDOW // tq + 1
    ncr = kc.shape[1]
    gw = GROUP * HD

    def win_spec(col, j):
        return pl.BlockSpec((tq, HD), lambda g, i: (jnp.maximum(i + j - (nwin - 1), 0), col + g))

    const = lambda *shape: pl.BlockSpec(shape, lambda g, i: (0,) * len(shape))
    in_specs = [
        pl.BlockSpec((tq, gw), lambda g, i: (i, COL_Q * LANES // gw + g)),
        pl.BlockSpec((1, ncr, HD), lambda g, i: (0, 0, g)),
        pl.BlockSpec((1, ncr, HD), lambda g, i: (0, 0, g)),
        pl.BlockSpec((T, HD), lambda g, i: (0, COL_KS + g)),
        pl.BlockSpec((HD, T), lambda g, i: (g, 0)),
    ]
    in_specs += [win_spec(COL_KW, j) for j in range(nwin)]
    in_specs += [win_spec(COL_VW, j) for j in range(nwin)]
    in_specs += [pl.BlockSpec((tq, LANES), lambda g, i: (i, COL_SMALL)), const(*A.shape), const(*E.shape)]
    return pl.pallas_call(
        functools.partial(_nsa_prompt_kernel, tq=tq, tk=tk, nwin=nwin),
        grid=(KV_HEADS, T // tq),
        in_specs=in_specs,
        out_specs=pl.BlockSpec((tq, gw), lambda g, i: (i, g)),
        out_shape=jax.ShapeDtypeStruct((T, KV_HEADS * gw), BF16),
        compiler_params=_cparams("arbitrary", "arbitrary"),
        name="nsa_prompt",
    )(z16, kc, vc, z16, vsT, *([z16] * (2 * nwin)), z32, A, E)


def _nsa_sample_kernel(pt_ref, *refs, G, T, past, wb):
    kpages = refs[:G]
    vpages = refs[G:2 * G]
    (z_q, z_ks, z_vs, z_kw, z_vw, z_gate, kc_ref, vc_ref, kwin_ref, vwin_ref, A_ref, E_ref,
     o_ref, kwo_ref, vwo_ref,
     sel_ref, ocmp_ref, m_ref, l_ref, acc_ref, pad_ref) = refs[2 * G:]
    kt = pl.program_id(1)
    nkt = pl.num_programs(1)
    tk = G * PAGE_SIZE
    R = GROUP * T
    gw = GROUP * HD
    t_col = lax.broadcasted_iota(jnp.int32, (T, 1), 0)
    qpos4 = _tile_rows(past + t_col)
    qall = z_q[...].astype(BF16)
    qs = [_stack_heads(qall[:, g * gw:(g + 1) * gw]) for g in range(KV_HEADS)]

    @pl.when(kt == 0)
    def _():
        nblk = A_ref.shape[1]
        blk = lax.broadcasted_iota(jnp.int32, (1, nblk), 1)
        cur4 = qpos4 // SEL_BLOCK
        forced = (blk == 0) | (blk == cur4) | (blk == cur4 - 1)
        for g in range(KV_HEADS):
            gs = slice(g * HD, (g + 1) * HD)
            kc = kc_ref[0, :, gs]
            ncr = kc.shape[0]
            cidx = lax.broadcasted_iota(jnp.int32, (1, ncr), 1)
            cok = (cidx >= 1) & (CMP_STRIDE * cidx + (CMP_BLOCK - CMP_STRIDE - 1) <= qpos4)
            p = _softmax_rows(_nt(qs[g], kc), cok)
            ocmp_ref[g] = jnp.dot(p.astype(BF16), vc_ref[0, :, gs], preferred_element_type=F32)
            psum = p[0:T]
            for r in range(1, GROUP):
                psum = psum + p[r * T:(r + 1) * T]
            imp = jnp.zeros((T, nblk), F32)
            for part in _split3(psum):
                imp = imp + jnp.dot(part, A_ref[...], preferred_element_type=F32)
            score = jnp.where(blk <= cur4, jnp.where(forced, BIG, _tile_rows(imp)), NEG)
            sel_ref[g] = _topk_mask(score, blk, N_SELECT)
        m_ref[...] = jnp.full(m_ref.shape, NEG, F32)
        l_ref[...] = jnp.zeros_like(l_ref)
        acc_ref[...] = jnp.zeros_like(acc_ref)

    def flash_update(g, s, ok, v):
        s = jnp.where(ok, s, 2.0 * NEG)
        m_i = m_ref[g]
        m_new = jnp.maximum(m_i, jnp.max(s, axis=-1, keepdims=True))
        alpha = jnp.exp2(m_i - m_new)
        pe = jnp.exp2(s - m_new)
        l_ref[g] = alpha * l_ref[g] + jnp.sum(pe, axis=-1, keepdims=True)
        acc_ref[g] = alpha * acc_ref[g] + jnp.dot(pe.astype(BF16), v, preferred_element_type=F32)
        m_ref[g] = m_new

    bpt = tk // SEL_BLOCK
    tiles_per_vreg = LANES // bpt
    nblk = A_ref.shape[1]
    def head_rows(pages, g):
        return jnp.concatenate([p[pl.ds(g, PAGE_SIZE, stride=KV_HEADS), :] for p in pages], axis=0).astype(BF16)

    kpos = kt * tk + lax.broadcasted_iota(jnp.int32, (1, tk), 1)
    shift = (LANES - bpt * (kt % tiles_per_vreg)) % LANES
    for g in range(KV_HEADS):
        gs = slice(g * HD, (g + 1) * HD)
        sel = sel_ref[g]
        half = sel[:, 0:LANES]
        for c in range(1, nblk // LANES):
            half = jnp.where(kt // tiles_per_vreg == c, sel[:, c * LANES:(c + 1) * LANES], half)
        moved = pltpu.roll(half, shift, axis=1)
        selx = jnp.dot(moved.astype(BF16), E_ref[...], preferred_element_type=F32)
        ok = (selx > 0.5) & (kpos <= qpos4)
        flash_update(g, _nt(qs[g], head_rows(kpages, g)), ok, head_rows(vpages, g))

    @pl.when(kt == nkt - 1)
    def _():
        n_past = past // SEL_BLOCK
        lane = lax.broadcasted_iota(jnp.int32, (1, LANES), 1)
        gt = jax.nn.sigmoid(z_gate[...])
        kw_new = z_kw[...]
        vw_new = z_vw[...]
        nk = KV_HEADS
        kwo_ref[0:nk * (wb - T), :] = kwin_ref[nk * T:nk * wb, :]
        vwo_ref[0:nk * (wb - T), :] = vwin_ref[nk * T:nk * wb, :]
        for g in range(nk):
            kwo_ref[pl.ds(nk * (wb - T) + g, T, stride=nk), :] = kw_new[:, g * HD:(g + 1) * HD]
            vwo_ref[pl.ds(nk * (wb - T) + g, T, stride=nk), :] = vw_new[:, g * HD:(g + 1) * HD]

        def padded(rows):
            pad_ref[...] = jnp.zeros_like(pad_ref)
            pad_ref[0:T, :] = rows
            return pad_ref[...].astype(BF16)

        ks_new = padded(z_ks[...])
        vs_new = padded(z_vs[...])
        kw_pad = padded(kw_new)
        vw_pad = padded(vw_new)
        for g in range(KV_HEADS):
            gs = slice(g * HD, (g + 1) * HD)
            q = qs[g]
            sel_tail = sel_ref[g][:, n_past:n_past + 1]
            ok = (sel_tail > 0.5) & (past + lane <= qpos4) & (lane < T)
            flash_update(g, _nt(q, ks_new[:, gs]), ok, vs_new[:, gs])
            o_slc = acc_ref[g] / l_ref[g]
            widx = lax.broadcasted_iota(jnp.int32, (1, wb), 1)
            rel_c = qpos4 - (past - wb + widx)
            ok_c = (rel_c >= 0) & (rel_c < WINDOW) & (past - wb + widx >= 0)
            rel_n = qpos4 - (past + lane)
            ok_n = (rel_n >= 0) & (rel_n < WINDOW) & (lane < T)
            s_c = _nt(q, kwin_ref[pl.ds(g, wb, stride=KV_HEADS), :].astype(BF16))
            s_n = _nt(q, kw_pad[:, gs])
            pw = _softmax_rows(jnp.concatenate([s_c, s_n], axis=1), jnp.concatenate([ok_c, ok_n], axis=1))
            o_win = jnp.dot(pw[:, :wb].astype(BF16), vwin_ref[pl.ds(g, wb, stride=KV_HEADS), :].astype(BF16),
                            preferred_element_type=F32)
            o_win = o_win + jnp.dot(pw[:, wb:].astype(BF16), vw_pad[:, gs], preferred_element_type=F32)
            o_cmp = ocmp_ref[g]
            for r in range(GROUP):
                rs = slice(r * T, (r + 1) * T)
                c0 = (g * GROUP + r) * 3
                o = (gt[:, c0:c0 + 1] * o_cmp[rs] + gt[:, c0 + 1:c0 + 2] * o_slc[rs]
                     + gt[:, c0 + 2:c0 + 3] * o_win[rs])
                o_ref[:, (g * GROUP + r) * HD:(g * GROUP + r + 1) * HD] = o


def _nsa_sample(z32, kc, vc, pool_k, pool_v, kwin, vwin, page_table, A, E, *, B, T, G):
    NP = page_table.shape[1]
    past = NP * PAGE_SIZE
    wb = kwin.shape[0] // (B * KV_HEADS)
    ncr = kc.shape[1]
    kvw = KV_HEADS * HD
    R = GROUP * T
    nblk = A.shape[1]

    def page_spec(k):
        return pl.BlockSpec((KV_HEADS * PAGE_SIZE, HD), lambda b, kt, pt: (pt[b, kt * G + k], 0))

    win_spec = pl.BlockSpec((KV_HEADS * wb, HD), lambda b, kt, pt: (b, 0))

    def zcol(width, col):
        return pl.BlockSpec((T, width), lambda b, kt, pt: (b, col * LANES // width))

    const = lambda *shape: pl.BlockSpec(shape, lambda b, kt, pt: (0,) * len(shape))
    per_b = lambda *shape: pl.BlockSpec((1,) + shape, lambda b, kt, pt: (b,) + (0,) * len(shape))
    in_specs = [page_spec(k) for k in range(G)] * 2 + [
        zcol(A_HEADS * HD, COL_Q), zcol(kvw, COL_KS), zcol(kvw, COL_VS), zcol(kvw, COL_KW), zcol(kvw, COL_VW),
        zcol(LANES, COL_SMALL), per_b(ncr, kvw), per_b(ncr, kvw), win_spec, win_spec,
        const(*A.shape), const(*E.shape)]
    return pl.pallas_call(
        functools.partial(_nsa_sample_kernel, G=G, T=T, past=past, wb=wb),
        grid_spec=pltpu.PrefetchScalarGridSpec(
            num_scalar_prefetch=1,
            grid=(B, NP // G),
            in_specs=in_specs,
            out_specs=[pl.BlockSpec((T, A_HEADS * HD), lambda b, kt, pt: (b, 0)), win_spec, win_spec],
            scratch_shapes=[
                pltpu.VMEM((KV_HEADS, R, nblk), F32), pltpu.VMEM((KV_HEADS, R, HD), F32),
                pltpu.VMEM((KV_HEADS, R, 1), F32), pltpu.VMEM((KV_HEADS, R, 1), F32),
                pltpu.VMEM((KV_HEADS, R, HD), F32), pltpu.VMEM((LANES, kvw), F32)],
        ),
        out_shape=[jax.ShapeDtypeStruct((B * T, A_HEADS * HD), F32),
                   jax.ShapeDtypeStruct(kwin.shape, F32), jax.ShapeDtypeStruct(vwin.shape, F32)],
        compiler_params=_cparams("arbitrary", "arbitrary"),
        name="nsa_sample",
    )(page_table, *([pool_k] * G), *([pool_v] * G), z32, z32, z32, z32, z32, z32, kc, vc, kwin, vwin, A, E)


def _sel_matrices(ncr, nblk_pad, tk):
    ratio = SEL_BLOCK // CMP_STRIDE
    lead = CMP_BLOCK // CMP_STRIDE - 1
    c = jnp.arange(ncr)[:, None] - 1
    j = jnp.arange(nblk_pad)[None, :]
    A = ((c >= 0) & (c >= ratio * j - lead) & (c <= ratio * j + ratio - 1)).astype(BF16)
    jj = jnp.arange(LANES)[:, None]
    cc = jnp.arange(tk)[None, :]
    E = ((jj == cc // SEL_BLOCK) & (jj < tk // SEL_BLOCK)).astype(BF16)
    return A, E


def _pack_params(p):
    d_nsa = A_HEADS * HD
    dm = M_HEADS * HD
    q_end = d_nsa
    kv_end = q_end + 6 * KV_HEADS * HD
    g_end = kv_end + 3 * A_HEADS
    qk_end = g_end + 2 * dm
    v_end = qk_end + dm
    if_end = v_end + 2 * M_HEADS
    w_in = p['w_in']
    d_model = w_in.shape[0]
    pad = N_PACKED - (COL_SMALL * LANES + 3 * A_HEADS + 2 * M_HEADS)
    w_packed = jnp.concatenate([
        w_in[:, g_end:qk_end], w_in[:, :q_end], w_in[:, qk_end:v_end], w_in[:, if_end:],
        w_in[:, q_end:kv_end], w_in[:, kv_end:g_end], w_in[:, v_end:if_end],
        jnp.zeros((d_model, pad), w_in.dtype)], axis=1).astype(BF16)
    gain = jnp.ones((N_PACKED,), F32)
    flag = jnp.zeros((N_PACKED,), F32)

    def put(vec, flg, col, g, reps):
        lo = col * LANES
        vec = vec.at[lo:lo + reps * HD].set(jnp.tile(g, reps))
        flg = flg.at[lo:lo + reps * HD].set(1.0)
        return vec, flg

    gain, flag = put(gain, flag, COL_Q, p['g_q'] * (HD ** -0.5 * math.log2(math.e)), A_HEADS)
    gain, flag = put(gain, flag, COL_KS, p['g_ks'], KV_HEADS)
    gain, flag = put(gain, flag, COL_KW, p['g_kw'], KV_HEADS)
    gbias = jnp.zeros((LANES,), F32)
    gbias = gbias.at[IG_LANE:IG_LANE + M_HEADS].set(p['b_i'])
    gbias = gbias.at[FG_LANE:FG_LANE + M_HEADS].set(p['b_f'])
    hrow = jnp.arange(2 * M_HEADS)[:, None]
    lane = jnp.arange(LANES)[None, :]
    sel = jnp.where(hrow < M_HEADS, lane == IG_LANE + hrow, lane == FG_LANE + hrow - M_HEADS).astype(BF16)
    half = CMP_STRIDE * HD

    def w1cat(w1):
        return jnp.concatenate([w1[:half], w1[half:]], axis=1).astype(BF16)

    return dict(
        g_attn=p['g_attn'].reshape(1, -1), w_packed=w_packed, gain=gain.reshape(1, -1), flag=flag.reshape(1, -1),
        w_conv=p['w_conv'], b_conv=p['b_conv'].reshape(1, -1), gbias=gbias.reshape(1, -1), sel=sel,
        g_mnorm=p['g_mnorm'],
        w1cat_k=w1cat(p['w_phi1_k']), w1cat_v=w1cat(p['w_phi1_v']),
        cb_k=_pe_bias(p['pe_k'], p['w_phi1_k']), cb_v=_pe_bias(p['pe_v'], p['w_phi1_v']),
        w2_k=p['w_phi2_k'].astype(BF16), w2_v=p['w_phi2_v'].astype(BF16), g_kc=p['g_kc'].reshape(1, -1),
        wo_a=p['w_out'][:d_nsa].astype(BF16), wo_m=p['w_out'][d_nsa:].astype(BF16),
        g_mlp=p['g_mlp'].reshape(1, -1), w_up=p['w_up'].astype(BF16), w_down=p['w_down'].astype(BF16),
    )


def _kv_cols(z32, col, n):
    return z32[:, col * LANES:(col + n) * LANES]


def _prompt_pass(x, pk):
    T = x.shape[0]
    dm = M_HEADS * HD
    tm = min(T, 1024)
    z32, z16 = _inproj(x, pk['g_attn'], pk['w_packed'], pk['gain'], pk['flag'], tm=tm)
    kc_raw = _kv_cols(z32, COL_KC, KV_HEADS)
    vc_raw = _kv_cols(z32, COL_VC, KV_HEADS)
    ks = _kv_cols(z32, COL_KS, KV_HEADS)
    vs = _kv_cols(z32, COL_VS, KV_HEADS)
    wb = min(WINDOW, T)
    kw = _kv_cols(z32, COL_KW, KV_HEADS)[T - wb:]
    vw = _kv_cols(z32, COL_VW, KV_HEADS)[T - wb:]

    n_pages = T // PAGE_SIZE
    ident = jnp.arange(n_pages, dtype=jnp.int32).reshape(1, n_pages)
    G = min(16, n_pages)
    kc = _compress(kc_raw.reshape(-1, HD), ident, pk['w1cat_k'], pk['cb_k'], pk['w2_k'], pk['g_kc'],
                   normalize=True, G=G)
    vc = _compress(vc_raw.reshape(-1, HD), ident, pk['w1cat_v'], pk['cb_v'], pk['w2_v'], pk['g_kc'],
                   normalize=False, G=G)
    tk = min(T, 1024)
    tq = 256
    nblk_pad = -(-(T // SEL_BLOCK) // LANES) * LANES
    A, E = _sel_matrices(kc.shape[1], nblk_pad, tk)
    vsT = _kv_cols(z16, COL_VS, KV_HEADS).T
    a_out = _nsa_prompt(z32, z16, vsT, kc, vc, A, (2.0 * NEG * E.astype(F32)).astype(BF16).T, T=T, tq=tq, tk=tk)

    L = min(T, 256)
    zeros = lambda *s: jnp.zeros(s, F32)
    m_out, conv, C, n, m = _mlstm(
        z32, z16, zeros(1, CONV_W - 1, 2 * dm), zeros(1, M_HEADS, HD, HD), zeros(1, M_HEADS, HD),
        zeros(1, M_HEADS), pk['w_conv'], pk['b_conv'], pk['gbias'], pk['sel'], pk['g_mnorm'],
        B=1, T=T, L=L, valid=L)
    x1 = _outproj(x, a_out, m_out[0], pk['wo_a'], pk['wo_m'], tm=min(T, 512))
    y = _mlp(x1, pk['g_mlp'], pk['w_up'], pk['w_down'], tm=min(T, 512))
    state = (kc_raw.reshape(1, T, KV_HEADS, HD), vc_raw.reshape(1, T, KV_HEADS, HD),
             ks.reshape(1, T, KV_HEADS, HD), vs.reshape(1, T, KV_HEADS, HD),
             kw.reshape(1, wb, KV_HEADS, HD), vw.reshape(1, wb, KV_HEADS, HD),
             conv, C, n, m.reshape(1, M_HEADS))
    return y, state


def _sample_pass(x, caches, states, page_table, pk):
    B, T, d_model = x.shape
    ck_cmp, cv_cmp, ck_slc, cv_slc, ck_win, cv_win = caches
    s_conv, s_C, s_n, s_m = states
    NP = page_table.shape[1]
    wb = ck_win.shape[1]
    xf = x.reshape(B * T, d_model)
    z32, _ = _inproj(xf, pk['g_attn'], pk['w_packed'], pk['gain'], pk['flag'], tm=B * T)

    assert (NP * PAGE_SIZE + T) // CMP_STRIDE == NP * PAGE_SIZE // CMP_STRIDE
    G = min(32, NP)
    Gc = min(32, NP)
    kc = _compress(ck_cmp.reshape(-1, HD), page_table, pk['w1cat_k'], pk['cb_k'], pk['w2_k'],
                   pk['g_kc'], normalize=True, G=Gc)
    vc = _compress(cv_cmp.reshape(-1, HD), page_table, pk['w1cat_v'], pk['cb_v'], pk['w2_v'],
                   pk['g_kc'], normalize=False, G=Gc)
    n_sel = NP * PAGE_SIZE // SEL_BLOCK + -(-T // SEL_BLOCK)
    nblk_pad = -(-n_sel // LANES) * LANES
    A, E = _sel_matrices(kc.shape[1], nblk_pad, G * PAGE_SIZE)
    a_out, kw_o, vw_o = _nsa_sample(
        z32, kc, vc, ck_slc.reshape(-1, HD), cv_slc.reshape(-1, HD),
        ck_win.reshape(-1, HD), cv_win.reshape(-1, HD), page_table, A, E, B=B, T=T, G=G)

    m_out, conv, C, n, m = _mlstm(
        z32, z32, s_conv, s_C, s_n, s_m, pk['w_conv'], pk['b_conv'], pk['gbias'], pk['sel'], pk['g_mnorm'],
        B=B, T=T, L=LANES, valid=T)
    x1 = _outproj(xf, a_out, m_out.reshape(B * T, -1), pk['wo_a'], pk['wo_m'], tm=B * T)
    y = _mlp(x1, pk['g_mlp'], pk['w_up'], pk['w_down'], tm=B * T)
    rows = lambda col: _kv_cols(z32, col, KV_HEADS).reshape(B, T, KV_HEADS, HD)
    state = (rows(COL_KC), rows(COL_VC), rows(COL_KS), rows(COL_VS),
             kw_o.reshape(B, wb, KV_HEADS, HD), vw_o.reshape(B, wb, KV_HEADS, HD),
             conv, C, n, m.reshape(B, M_HEADS))
    return y.reshape(B, T, d_model), state


_PARAM_NAMES = ('g_attn', 'w_in', 'w_conv', 'b_conv', 'b_i', 'b_f', 'g_mnorm', 'g_q', 'g_ks', 'g_kw', 'g_kc',
                'w_phi1_k', 'w_phi2_k', 'pe_k', 'w_phi1_v', 'w_phi2_v', 'pe_v', 'w_out', 'g_mlp', 'w_up',
                'w_down')


def kernel(x_prompt, x_sample, cache_k_cmp, cache_v_cmp, cache_k_slc, cache_v_slc, cache_k_win, cache_v_win,
           state_conv, state_C, state_n, state_m, page_table, g_attn, w_in, w_conv, b_conv, b_i, b_f, g_mnorm,
           g_q, g_ks, g_kw, g_kc, w_phi1_k, w_phi2_k, pe_k, w_phi1_v, w_phi2_v, pe_v, w_out, g_mlp, w_up,
           w_down):
    weights = (g_attn, w_in, w_conv, b_conv, b_i, b_f, g_mnorm, g_q, g_ks, g_kw, g_kc, w_phi1_k, w_phi2_k,
               pe_k, w_phi1_v, w_phi2_v, pe_v, w_out, g_mlp, w_up, w_down)
    depth = w_in.shape[0]
    assert depth == 1 and x_prompt.shape[0] == 1
    pk = _pack_params({k: w[0] for k, w in zip(_PARAM_NAMES, weights)})
    yp, st_p = _prompt_pass(x_prompt[0], pk)
    ys, st_s = _sample_pass(
        x_sample, (cache_k_cmp[0], cache_v_cmp[0], cache_k_slc[0], cache_v_slc[0], cache_k_win[0], cache_v_win[0]),
        (state_conv[0], state_C[0], state_n[0], state_m[0]), page_table, pk)
    return (yp[None], ys) + tuple(s[None] for s in st_p) + tuple(s[None] for s in st_s)
```

```python
import functools
import math

import jax
import jax.numpy as jnp
from jax import lax
from jax.experimental import pallas as pl
from jax.experimental.pallas import tpu as pltpu

F32 = jnp.float32
BF16 = jnp.bfloat16

EPS = 1e-6
NEG = -1e30
BIG = 1e9

M_HEADS = 8
A_HEADS = 8
KV_HEADS = 2
GROUP = A_HEADS // KV_HEADS
HD = 128
CONV_W = 4
CMP_BLOCK = 32
CMP_STRIDE = 16
CMP_HIDDEN = 2 * HD
SEL_BLOCK = 64
N_SELECT = 16
WINDOW = 512
PAGE_SIZE = 128

LANES = 128
VMEM_LIMIT_BYTES = 56 * 1024 * 1024

COL_MQK = 0
COL_Q = 16
COL_MV = 24
COL_MO = 32
COL_KC = 40
COL_VC = 42
COL_KS = 44
COL_VS = 46
COL_KW = 48
COL_VW = 50
COL_SMALL = 52
N_PACKED = 56 * LANES
GATE_LANE = 0
IG_LANE = 24
FG_LANE = 32


def _cparams(*sem):
    return pltpu.CompilerParams(dimension_semantics=sem, vmem_limit_bytes=VMEM_LIMIT_BYTES)


def _nt(a, b):
    return lax.dot_general(a, b, (((1,), (1,)), ((), ())), preferred_element_type=F32)


def _split3(x):
    x1 = x.astype(BF16)
    r1 = x - x1.astype(F32)
    x2 = r1.astype(BF16)
    r2 = r1 - x2.astype(F32)
    return x1, x2, r2.astype(BF16)


def _inproj_kernel(x_ref, g_ref, w_ref, gain_ref, flag_ref, z32_ref, z16_ref, h_ref, *, norm_tiles):
    j = pl.program_id(1)

    @pl.when(j == 0)
    def _():
        x = x_ref[...]
        ms = jnp.mean(x * x, axis=-1, keepdims=True)
        h_ref[...] = (x * lax.rsqrt(ms + EPS) * g_ref[...]).astype(BF16)

    z = jnp.dot(h_ref[...], w_ref[...], preferred_element_type=F32)
    is_norm = functools.reduce(jnp.logical_or, [j == t for t in norm_tiles])

    @pl.when(is_norm)
    def _():
        tn = z.shape[1]
        for c in range(tn // LANES):
            sl = slice(c * LANES, (c + 1) * LANES)
            zc = z[:, sl]
            nrm = zc * lax.rsqrt(jnp.mean(zc * zc, axis=-1, keepdims=True) + EPS) * gain_ref[:, sl]
            out = jnp.where(flag_ref[:, sl] > 0.5, nrm, zc)
            z32_ref[:, sl] = out
            z16_ref[:, sl] = out.astype(BF16)

    @pl.when(jnp.logical_not(is_norm))
    def _():
        z32_ref[...] = z
        z16_ref[...] = z.astype(BF16)


def _inproj(x, g, w, gain, flag, *, tm, tn=1024):
    M, K = x.shape
    N = w.shape[1]
    norm_tiles = tuple(sorted({c * LANES // tn for c in (COL_Q, COL_Q + A_HEADS - 1, COL_KS, COL_KW)}))
    return pl.pallas_call(
        functools.partial(_inproj_kernel, norm_tiles=norm_tiles),
        grid=(M // tm, N // tn),
        in_specs=[
            pl.BlockSpec((tm, K), lambda i, j: (i, 0)),
            pl.BlockSpec((1, K), lambda i, j: (0, 0)),
            pl.BlockSpec((K, tn), lambda i, j: (0, j)),
            pl.BlockSpec((1, tn), lambda i, j: (0, j)),
            pl.BlockSpec((1, tn), lambda i, j: (0, j)),
        ],
        out_specs=[
            pl.BlockSpec((tm, tn), lambda i, j: (i, j)),
            pl.BlockSpec((tm, tn), lambda i, j: (i, j)),
        ],
        out_shape=[jax.ShapeDtypeStruct((M, N), F32), jax.ShapeDtypeStruct((M, N), BF16)],
        scratch_shapes=[pltpu.VMEM((tm, K), BF16)],
        compiler_params=_cparams("arbitrary", "arbitrary"),
        name="inproj",
    )(x, g, w, gain, flag)


def _outproj_kernel(x_ref, a_ref, m_ref, wa_ref, wm_ref, o_ref):
    acc = jnp.dot(a_ref[...].astype(BF16), wa_ref[...], preferred_element_type=F32)
    acc = acc + jnp.dot(m_ref[...].astype(BF16), wm_ref[...], preferred_element_type=F32)
    o_ref[...] = x_ref[...] + acc


def _outproj(x, a, m, wa, wm, *, tm, tn=2048):
    M, N = x.shape
    Ka, Km = a.shape[1], m.shape[1]
    return pl.pallas_call(
        _outproj_kernel,
        grid=(M // tm, N // tn),
        in_specs=[
            pl.BlockSpec((tm, tn), lambda i, j: (i, j)),
            pl.BlockSpec((tm, Ka), lambda i, j: (i, 0)),
            pl.BlockSpec((tm, Km), lambda i, j: (i, 0)),
            pl.BlockSpec((Ka, tn), lambda i, j: (0, j)),
            pl.BlockSpec((Km, tn), lambda i, j: (0, j)),
        ],
        out_specs=pl.BlockSpec((tm, tn), lambda i, j: (i, j)),
        out_shape=jax.ShapeDtypeStruct((M, N), F32),
        compiler_params=_cparams("arbitrary", "arbitrary"),
        name="outproj",
    )(x, a, m, wa, wm)


def _mlp_kernel(x_ref, g_ref, wu_ref, wd_ref, o_ref, h_ref, acc_ref):
    k = pl.program_id(1)

    @pl.when(k == 0)
    def _():
        x = x_ref[...]
        ms = jnp.mean(x * x, axis=-1, keepdims=True)
        h_ref[...] = (x * lax.rsqrt(ms + EPS) * g_ref[...]).astype(BF16)
        acc_ref[...] = jnp.zeros_like(acc_ref)

    u = jnp.dot(h_ref[...], wu_ref[...], preferred_element_type=F32)
    u = jnp.square(jnp.maximum(u, 0.0)).astype(BF16)
    acc_ref[...] += jnp.dot(u, wd_ref[...], preferred_element_type=F32)

    @pl.when(k == pl.num_programs(1) - 1)
    def _():
        o_ref[...] = x_ref[...] + acc_ref[...]


def _mlp(x, g, wu, wd, *, tm, tf=1024):
    M, D = x.shape
    FF = wu.shape[1]
    return pl.pallas_call(
        _mlp_kernel,
        grid=(M // tm, FF // tf),
        in_specs=[
            pl.BlockSpec((tm, D), lambda i, k: (i, 0)),
            pl.BlockSpec((1, D), lambda i, k: (0, 0)),
            pl.BlockSpec((D, tf), lambda i, k: (0, k)),
            pl.BlockSpec((tf, D), lambda i, k: (k, 0)),
        ],
        out_specs=pl.BlockSpec((tm, D), lambda i, k: (i, 0)),
        out_shape=jax.ShapeDtypeStruct((M, D), F32),
        scratch_shapes=[pltpu.VMEM((tm, D), BF16), pltpu.VMEM((tm, D), F32)],
        compiler_params=_cparams("arbitrary", "arbitrary"),
        name="mlp",
    )(x, g, wu, wd)


def _silu(x):
    return x * jax.nn.sigmoid(x)


def _log_sigmoid(x):
    return jnp.minimum(x, 0.0) - jnp.log1p(jnp.exp(-jnp.abs(x)))


def _mlstm_kernel(mqk_ref, mv_ref, sm_ref, mo_ref, conv0_ref, C0_ref, n0_ref, m0_ref,
                  wconv_ref, bconv_ref, gbias_ref, sel_ref, gm_ref,
                  out_ref, conv_ref, C_ref, n_ref, m_ref,
                  xbuf_ref, *pad_refs, L, valid):
    c = pl.program_id(1)
    dm = M_HEADS * HD

    @pl.when(c == 0)
    def _():
        C_ref[...] = C0_ref[...]
        n_ref[...] = n0_ref[...]
        m_ref[...] = m0_ref[...]
        xbuf_ref[...] = jnp.zeros_like(xbuf_ref)
        xbuf_ref[8 - (CONV_W - 1):8, :] = conv0_ref[0]

    if valid < L:
        vbuf, sbuf, obuf = pad_refs
        vbuf[...] = jnp.zeros_like(vbuf)
        sbuf[...] = jnp.zeros_like(sbuf)
        obuf[...] = jnp.zeros_like(obuf)
        vbuf[0:valid, :] = mv_ref[0]
        sbuf[0:valid, :] = sm_ref[0]
        obuf[0:valid, :] = mo_ref[0]
        xbuf_ref[8:8 + valid, :] = mqk_ref[0]
        v_all, sm, mo = vbuf[...], sbuf[...], obuf[...]
    else:
        xbuf_ref[8:8 + L, :] = mqk_ref[0]
        v_all, sm, mo = mv_ref[0], sm_ref[0], mo_ref[0]

    w = wconv_ref[...]
    y = bconv_ref[...] + xbuf_ref[8:8 + L, :] * w[CONV_W - 1:CONV_W, :]
    for j in range(CONV_W - 1):
        y = y + xbuf_ref[5 + j:5 + j + L, :] * w[j:j + 1, :]
    qk = _silu(y)
    tail = xbuf_ref[5 + valid:8 + valid, :]
    conv_ref[0] = tail
    xbuf_ref[5:8, :] = tail

    row = lax.broadcasted_iota(jnp.int32, (L, 1), 0)
    live = row < valid
    gt = sm + gbias_ref[...]
    ig_all = jnp.where(live, gt, NEG)
    lf_all = jnp.where(live, _log_sigmoid(gt), 0.0)
    F_all = lf_all
    sh = 1
    while sh < L:
        F_all = F_all + jnp.where(row >= sh, pltpu.roll(F_all, sh, axis=0), 0.0)
        sh *= 2
    sel = sel_ref[...]
    head_row = lax.broadcasted_iota(jnp.int32, (2 * M_HEADS, 1), 0)
    rows = jnp.zeros((2 * M_HEADS, L), F32)
    for a, b in zip(_split3(ig_all), _split3(F_all)):
        rows = rows + jnp.where(head_row < M_HEADS, _nt(sel, a), _nt(sel, b))

    tt = lax.broadcasted_iota(jnp.int32, (L, L), 0)
    ss = lax.broadcasted_iota(jnp.int32, (L, L), 1)
    tri = ss <= tt
    for h in range(M_HEADS):
        hs = slice(h * HD, (h + 1) * HD)
        F_c = F_all[:, FG_LANE + h:FG_LANE + h + 1]
        ig_c = ig_all[:, IG_LANE + h:IG_LANE + h + 1]
        ig_r = rows[h:h + 1, :]
        F_r = rows[M_HEADS + h:M_HEADS + h + 1, :]
        m_h = m_ref[0, :, h:h + 1]
        D = jnp.where(tri, ig_r + F_c - F_r, NEG)
        a_c = m_h + F_c
        mt = jnp.maximum(a_c, jnp.max(D, axis=-1, keepdims=True))
        qf = qk[:, hs]
        kf = qk[:, dm + h * HD:dm + (h + 1) * HD] * (HD ** -0.5)
        qb, kb = qf.astype(BF16), kf.astype(BF16)
        vb = v_all[:, hs].astype(BF16)
        S = _nt(qb, kb) * jnp.exp(D - mt)
        inter = jnp.exp(a_c - mt)
        Ch = C_ref[0, h]
        nh = n_ref[0, h:h + 1, :]
        num = jnp.dot(S.astype(BF16), vb, preferred_element_type=F32) + inter * jnp.dot(
            qb, Ch.astype(BF16), preferred_element_type=F32)
        den = jnp.sum(S, axis=-1, keepdims=True) + inter * jnp.sum(qf * nh, axis=-1, keepdims=True)
        hh = num / jnp.maximum(jnp.abs(den), jnp.exp(-mt))
        m_new = mt[valid - 1:valid, :]
        F_last = F_c[valid - 1:valid, :]
        w_c = jnp.exp(ig_c + F_last - F_c - m_new)
        decay = jnp.exp(a_c[valid - 1:valid, :] - m_new)
        kw = kf * w_c
        C_ref[0, h] = decay * Ch + lax.dot_general(
            kw.astype(BF16), vb, (((0,), (0,)), ((), ())), preferred_element_type=F32)
        n_ref[0, h:h + 1, :] = decay * nh + jnp.sum(kw, axis=0, keepdims=True)
        m_ref[0, :, h:h + 1] = m_new
        hn = hh * lax.rsqrt(jnp.mean(hh * hh, axis=-1, keepdims=True) + EPS) * gm_ref[h:h + 1, :]
        res = (hn * jax.nn.sigmoid(mo[:, hs])).astype(out_ref.dtype)
        out_ref[0, :, hs] = res[0:valid, :] if valid < L else res


def _mlstm(z32, z16, conv0, C0, n0, m0, wconv, bconv, gbias, sel, gm, *, B, T, L, valid):
    dm = M_HEADS * HD
    rows = valid if valid < L else L
    nchunk = T // rows
    z32 = z32.reshape(B, T, N_PACKED)
    z16 = z16.reshape(B, T, N_PACKED)
    scratch = [pltpu.VMEM((L + 8, 2 * dm), F32)]
    if valid < L:
        scratch += [pltpu.VMEM((L, dm), F32), pltpu.VMEM((L, LANES), F32), pltpu.VMEM((L, dm), F32)]
    zv = z32 if valid < L else z16
    out_dtype = F32 if valid < L else BF16
    const = lambda *shape: pl.BlockSpec(shape, lambda b, c: (0,) * len(shape))
    return pl.pallas_call(
        functools.partial(_mlstm_kernel, L=L, valid=valid),
        grid=(B, nchunk),
        in_specs=[
            pl.BlockSpec((1, rows, 2 * dm), lambda b, c: (b, c, COL_MQK * LANES // (2 * dm))),
            pl.BlockSpec((1, rows, dm), lambda b, c: (b, c, COL_MV * LANES // dm)),
            pl.BlockSpec((1, rows, LANES), lambda b, c: (b, c, COL_SMALL)),
            pl.BlockSpec((1, rows, dm), lambda b, c: (b, c, COL_MO * LANES // dm)),
            pl.BlockSpec((1, CONV_W - 1, 2 * dm), lambda b, c: (b, 0, 0)),
            pl.BlockSpec((1, M_HEADS, HD, HD), lambda b, c: (b, 0, 0, 0)),
            pl.BlockSpec((1, M_HEADS, HD), lambda b, c: (b, 0, 0)),
            pl.BlockSpec((1, 1, M_HEADS), lambda b, c: (b, 0, 0)),
            const(CONV_W, 2 * dm), const(1, 2 * dm), const(1, LANES), const(2 * M_HEADS, LANES),
            const(M_HEADS, HD),
        ],
        out_specs=[
            pl.BlockSpec((1, rows, dm), lambda b, c: (b, c, 0)),
            pl.BlockSpec((1, CONV_W - 1, 2 * dm), lambda b, c: (b, 0, 0)),
            pl.BlockSpec((1, M_HEADS, HD, HD), lambda b, c: (b, 0, 0, 0)),
            pl.BlockSpec((1, M_HEADS, HD), lambda b, c: (b, 0, 0)),
            pl.BlockSpec((1, 1, M_HEADS), lambda b, c: (b, 0, 0)),
        ],
        out_shape=[
            jax.ShapeDtypeStruct((B, T, dm), out_dtype),
            jax.ShapeDtypeStruct((B, CONV_W - 1, 2 * dm), F32),
            jax.ShapeDtypeStruct((B, M_HEADS, HD, HD), F32),
            jax.ShapeDtypeStruct((B, M_HEADS, HD), F32),
            jax.ShapeDtypeStruct((B, 1, M_HEADS), F32),
        ],
        scratch_shapes=scratch,
        compiler_params=_cparams("arbitrary", "arbitrary"),
        name="mlstm",
    )(z32, zv, z32, z32, conv0, C0, n0, m0.reshape(B, 1, M_HEADS), wconv, bconv, gbias, sel, gm)


_PIECE_PITCH = KV_HEADS * CMP_STRIDE + 1


def _compress_kernel(pt_ref, *refs, G, normalize):
    pages = refs[:G]
    w1_ref, cb_ref, w2_ref, gain_ref, out_ref, carry_ref, pad_ref = refs[G:]
    pg = pl.program_id(1)
    ppp = PAGE_SIZE // CMP_STRIDE
    R = G * ppp

    @pl.when(pg == 0)
    def _():
        carry_ref[...] = jnp.zeros_like(carry_ref)

    piece = KV_HEADS * CMP_STRIDE
    for k, p in enumerate(pages):
        for q in range(ppp):
            pad_ref[pl.ds((k * ppp + q) * _PIECE_PITCH, piece), :] = p[q * piece:(q + 1) * piece, :]
    xs = []
    for g in range(KV_HEADS):
        for k in range(G):
            xs.append(jnp.concatenate(
                [pad_ref[pl.ds(k * ppp * _PIECE_PITCH + KV_HEADS * j + g, ppp, stride=_PIECE_PITCH), :]
                 for j in range(CMP_STRIDE)], axis=1))
    X2 = jnp.concatenate(xs, axis=0).astype(BF16)
    LH = jnp.dot(X2, w1_ref[...], preferred_element_type=F32)
    row = lax.broadcasted_iota(jnp.int32, (R, 1), 0)
    for g in range(KV_HEADS):
        lo = LH[g * R:(g + 1) * R, :CMP_HIDDEN]
        hi = LH[g * R:(g + 1) * R, CMP_HIDDEN:]
        lo_prev = jnp.where(row == 0, carry_ref[g:g + 1, :], pltpu.roll(lo, 1, axis=0))
        carry_ref[g:g + 1, :] = lo[R - 1:R, :]
        hid = lo_prev + hi + cb_ref[0:1, :]
        out = jnp.dot(_silu(hid).astype(BF16), w2_ref[...], preferred_element_type=F32)
        if normalize:
            out = out * lax.rsqrt(jnp.mean(out * out, axis=-1, keepdims=True) + EPS) * gain_ref[...]
        out_ref[0, :, g * HD:(g + 1) * HD] = out.astype(out_ref.dtype)


def _compress(pool, page_table, w1cat, cbias, w2, gain, *, normalize, G=16):
    B, NP = page_table.shape
    ppp = PAGE_SIZE // CMP_STRIDE
    R = G * ppp

    def page_spec(k):
        return pl.BlockSpec((KV_HEADS * PAGE_SIZE, HD), lambda b, pg, pt: (pt[b, pg * G + k], 0))

    const = lambda *shape: pl.BlockSpec(shape, lambda b, pg, pt: (0,) * len(shape))
    return pl.pallas_call(
        functools.partial(_compress_kernel, G=G, normalize=normalize),
        grid_spec=pltpu.PrefetchScalarGridSpec(
            num_scalar_prefetch=1,
            grid=(B, NP // G),
            in_specs=[page_spec(k) for k in range(G)] + [
                const(CMP_STRIDE * HD, 2 * CMP_HIDDEN), const(8, CMP_HIDDEN), const(CMP_HIDDEN, HD),
                const(1, HD)],
            out_specs=pl.BlockSpec((1, R, KV_HEADS * HD), lambda b, pg, pt: (b, pg, 0)),
            scratch_shapes=[pltpu.VMEM((8, CMP_HIDDEN), F32),
                            pltpu.VMEM((-(-G * ppp * _PIECE_PITCH // 8) * 8, HD), F32)],
        ),
        out_shape=jax.ShapeDtypeStruct((B, NP * ppp, KV_HEADS * HD), BF16),
        compiler_params=_cparams("arbitrary", "arbitrary"),
        name="compress",
    )(page_table, *([pool] * G), w1cat, cbias, w2, gain)


def _pe_bias_kernel(pe_ref, w1_ref, o_ref):
    o_ref[...] = jnp.dot(pe_ref[...].astype(BF16), w1_ref[...].astype(BF16), preferred_element_type=F32)


def _pe_bias(pe, w1):
    pe8 = jnp.broadcast_to(pe.reshape(1, -1), (8, pe.size))
    return pl.pallas_call(
        _pe_bias_kernel,
        out_shape=jax.ShapeDtypeStruct((8, w1.shape[1]), F32),
        compiler_params=pltpu.CompilerParams(vmem_limit_bytes=VMEM_LIMIT_BYTES),
        name="pe_bias",
    )(pe8, w1)


def _softmax_parts(s, ok):
    sm = jnp.where(ok, s, NEG)
    mx = jnp.max(sm, axis=-1, keepdims=True)
    e = jnp.exp2(sm - mx)
    den = jnp.sum(e, axis=-1, keepdims=True)
    return e, jnp.where(mx > 0.5 * NEG, 1.0 / den, 0.0)


def _topk_mask(score, blk, n_pick):
    width = float(score.shape[-1])
    blkf = blk.astype(F32)
    sel = jnp.zeros(score.shape, F32)
    for _ in range(n_pick):
        mx = jnp.max(score, axis=-1, keepdims=True)
        idx = jnp.min(jnp.where(score == mx, blkf, width), axis=-1, keepdims=True)
        hit = blkf == idx
        sel = jnp.where(hit, 1.0, sel)
        score = jnp.where(hit, -jnp.inf, score)
    return sel


def _topk_mask_cols(score_t, n_pick):
    width = float(score_t.shape[0])
    blkf = lax.broadcasted_iota(jnp.int32, (score_t.shape[0], 1), 0).astype(F32)
    sel = jnp.zeros(score_t.shape, F32)
    for _ in range(n_pick):
        mx = jnp.max(score_t, axis=0, keepdims=True)
        idx = jnp.min(jnp.where(score_t == mx, blkf, width), axis=0, keepdims=True)
        hit = blkf == idx
        sel = jnp.where(hit, 1.0, sel)
        score_t = jnp.where(hit, -jnp.inf, score_t)
    return sel


def _stack_heads(qf):
    return jnp.concatenate([qf[:, r * HD:(r + 1) * HD] for r in range(GROUP)], axis=0)


def _tile_rows(x):
    return jnp.concatenate([x] * GROUP, axis=0)


def _nsa_prompt_kernel(q_ref, kc_ref, vc_ref, ks_ref, vs_ref, *rest, tq, tk, nwin):
    kw_refs = rest[:nwin]
    vw_refs = rest[nwin:2 * nwin]
    gate_ref, A_ref, E_ref, o_ref = rest[2 * nwin:]
    g = pl.program_id(0)
    i = pl.program_id(1)
    q0 = i * tq
    q = _stack_heads(q_ref[...])
    qpos = q0 + lax.broadcasted_iota(jnp.int32, (tq, 1), 0)
    qpos4 = _tile_rows(qpos)

    kc = kc_ref[0]
    ncr = kc.shape[0]
    cidx = lax.broadcasted_iota(jnp.int32, (1, ncr), 1)
    cok = (cidx >= 1) & (CMP_STRIDE * cidx + (CMP_BLOCK - CMP_STRIDE - 1) <= qpos4)
    e, inv = _softmax_parts(_nt(q, kc), cok)
    o_cmp = jnp.dot(e.astype(BF16), vc_ref[0], preferred_element_type=F32) * inv
    psum = e[0:tq] * inv[0:tq]
    for r in range(1, GROUP):
        psum = psum + e[r * tq:(r + 1) * tq] * inv[r * tq:(r + 1) * tq]
    imp = jnp.zeros((tq, A_ref.shape[1]), F32)
    for part in _split3(psum):
        imp = imp + jnp.dot(part, A_ref[...], preferred_element_type=F32)
    nblk = A_ref.shape[1]
    blk = lax.broadcasted_iota(jnp.int32, (1, nblk), 1)
    cur = qpos // SEL_BLOCK
    forced = (blk == 0) | (blk == cur) | (blk == cur - 1)
    score = jnp.where(blk <= cur, jnp.where(forced, BIG, imp), NEG)
    s_w, ok_w = [], []
    for j in range(nwin):
        kwpos = q0 + (j - (nwin - 1)) * tq + lax.broadcasted_iota(jnp.int32, (1, tq), 1)
        rel = qpos4 - kwpos
        ok_w.append((rel >= 0) & (rel < WINDOW) & (kwpos >= 0))
        s_w.append(_nt(q, kw_refs[j][...]))
    ew, inv_w = _softmax_parts(jnp.concatenate(s_w, axis=1), jnp.concatenate(ok_w, axis=1))
    o_win = jnp.zeros((GROUP * tq, HD), F32)
    for j in range(nwin):
        o_win = o_win + jnp.dot(ew[:, j * tq:(j + 1) * tq].astype(BF16), vw_refs[j][...],
                                preferred_element_type=F32)
    o_win = o_win * inv_w

    notsel = 1.0 - _topk_mask_cols(score.T, N_SELECT).T

    bpt = tk // SEL_BLOCK
    tiles_per_vreg = LANES // bpt
    rows_q = GROUP * tq
    kpos_col = lax.broadcasted_iota(jnp.int32, (tk, 1), 0)
    qpos_row = q0 + lax.broadcasted_iota(jnp.int32, (1, rows_q), 1) % tq

    def body(kt, carry, causal):
        m_i, l_i, acc = carry
        start = pl.multiple_of(kt * tk, tk)
        half = notsel[:, 0:LANES]
        for c in range(1, nblk // LANES):
            half = jnp.where(kt // tiles_per_vreg == c, notsel[:, c * LANES:(c + 1) * LANES], half)
        shift = (LANES - bpt * (kt % tiles_per_vreg)) % LANES
        moved = pltpu.roll(half, shift, axis=1).astype(BF16)
        qa = jnp.concatenate([q, _tile_rows(moved)], axis=1)
        ka = jnp.concatenate([ks_ref[pl.ds(start, tk), :], E_ref[...]], axis=1)
        s = _nt(ka, qa)
        if causal:
            s = jnp.where(start + kpos_col <= qpos_row, s, 2.0 * NEG)
        m_new = jnp.maximum(m_i, jnp.max(s, axis=0, keepdims=True))
        alpha = jnp.exp2(m_i - m_new)
        pe = jnp.exp2(s - m_new)
        l_new = alpha * l_i + jnp.sum(pe, axis=0, keepdims=True)
        acc = alpha * acc + jnp.dot(vs_ref[:, pl.ds(start, tk)], pe.astype(BF16), preferred_element_type=F32)
        return m_new, l_new, acc

    nkt = (q0 + tq + tk - 1) // tk
    init = (jnp.full((1, rows_q), NEG, F32), jnp.zeros((1, rows_q), F32), jnp.zeros((HD, rows_q), F32))

    def body2(j, carry):
        return body(2 * j + 1, body(2 * j, carry, False), False)

    n_full = nkt - 1
    carry = lax.fori_loop(0, n_full // 2, body2, init)
    carry = lax.cond(n_full % 2 == 1, lambda c: body(n_full - 1, c, False), lambda c: c, carry)
    _, l_f, acc_f = body(nkt - 1, carry, True)
    o_slc = (acc_f / l_f).T

    gt = jax.nn.sigmoid(gate_ref[...])
    gt = jnp.where(g == 0, gt, pltpu.roll(gt, LANES - 3 * GROUP, axis=1))
    for r in range(GROUP):
        rs = slice(r * tq, (r + 1) * tq)
        o = (gt[:, 3 * r:3 * r + 1] * o_cmp[rs] + gt[:, 3 * r + 1:3 * r + 2] * o_slc[rs]
             + gt[:, 3 * r + 2:3 * r + 3] * o_win[rs])
        o_ref[:, r * HD:(r + 1) * HD] = o.astype(o_ref.dtype)


def _nsa_prompt(z32, z16, vsT, kc, vc, A, E, *, T, tq=128, tk=512):
    nwin = WINDOW // tq + 1
    ncr = kc.shape[1]
    gw = GROUP * HD

    def win_spec(col, j):
        return pl.BlockSpec((tq, HD), lambda g, i: (jnp.maximum(i + j - (nwin - 1), 0), col + g))

    const = lambda *shape: pl.BlockSpec(shape, lambda g, i: (0,) * len(shape))
    in_specs = [
        pl.BlockSpec((tq, gw), lambda g, i: (i, COL_Q * LANES // gw + g)),
        pl.BlockSpec((1, ncr, HD), lambda g, i: (0, 0, g)),
        pl.BlockSpec((1, ncr, HD), lambda g, i: (0, 0, g)),
        pl.BlockSpec((T, HD), lambda g, i: (0, COL_KS + g)),
        pl.BlockSpec((HD, T), lambda g, i: (g, 0)),
    ]
    in_specs += [win_spec(COL_KW, j) for j in range(nwin)]
    in_specs += [win_spec(COL_VW, j) for j in range(nwin)]
    in_specs += [pl.BlockSpec((tq, LANES), lambda g, i: (i, COL_SMALL)), const(*A.shape), const(*E.shape)]
    return pl.pallas_call(
        functools.partial(_nsa_prompt_kernel, tq=tq, tk=tk, nwin=nwin),
        grid=(KV_HEADS, T // tq),
        in_specs=in_specs,
        out_specs=pl.BlockSpec((tq, gw), lambda g, i: (i, g)),
        out_shape=jax.ShapeDtypeStruct((T, KV_HEADS * gw), BF16),
        compiler_params=_cparams("arbitrary", "arbitrary"),
        name="nsa_prompt",
    )(z16, kc, vc, z16, vsT, *([z16] * (2 * nwin)), z32, A, E)


def _nsa_sample_kernel(pt_ref, *refs, G, T, past, wb):
    kpages = refs[:G]
    vpages = refs[G:2 * G]
    (z_q, z_ks, z_vs, z_kw, z_vw, z_gate, kc_ref, vc_ref, kwin_ref, vwin_ref, A_ref, E_ref,
     o_ref, kwo_ref, vwo_ref,
     sel_ref, ocmp_ref, m_ref, l_ref, acc_ref, pad_ref) = refs[2 * G:]
    kt = pl.program_id(1)
    nkt = pl.num_programs(1)
    tk = G * PAGE_SIZE
    R = GROUP * T
    gw = GROUP * HD
    t_col = lax.broadcasted_iota(jnp.int32, (T, 1), 0)
    qpos4 = _tile_rows(past + t_col)
    qall = z_q[...].astype(BF16)
    qs = [_stack_heads(qall[:, g * gw:(g + 1) * gw]) for g in range(KV_HEADS)]

    @pl.when(kt == 0)
    def _():
        nblk = A_ref.shape[1]
        blk = lax.broadcasted_iota(jnp.int32, (1, nblk), 1)
        cur4 = qpos4 // SEL_BLOCK
        forced = (blk == 0) | (blk == cur4) | (blk == cur4 - 1)
        for g in range(KV_HEADS):
            gs = slice(g * HD, (g + 1) * HD)
            kc = kc_ref[0, :, gs]
            ncr = kc.shape[0]
            cidx = lax.broadcasted_iota(jnp.int32, (1, ncr), 1)
            cok = (cidx >= 1) & (CMP_STRIDE * cidx + (CMP_BLOCK - CMP_STRIDE - 1) <= qpos4)
            e, inv = _softmax_parts(_nt(qs[g], kc), cok)
            ocmp_ref[g] = jnp.dot(e.astype(BF16), vc_ref[0, :, gs], preferred_element_type=F32) * inv
            psum = e[0:T] * inv[0:T]
            for r in range(1, GROUP):
                psum = psum + e[r * T:(r + 1) * T] * inv[r * T:(r + 1) * T]
            imp = jnp.zeros((T, nblk), F32)
            for part in _split3(psum):
                imp = imp + jnp.dot(part, A_ref[...], preferred_element_type=F32)
            score = jnp.where(blk <= cur4, jnp.where(forced, BIG, _tile_rows(imp)), NEG)
            sel_ref[g] = _topk_mask(score, blk, N_SELECT)
        m_ref[...] = jnp.full(m_ref.shape, NEG, F32)
        l_ref[...] = jnp.zeros_like(l_ref)
        acc_ref[...] = jnp.zeros_like(acc_ref)

    def flash_update(g, s, ok, v):
        s = jnp.where(ok, s, 2.0 * NEG)
        m_i = m_ref[g]
        m_new = jnp.maximum(m_i, jnp.max(s, axis=-1, keepdims=True))
        alpha = jnp.exp2(m_i - m_new)
        pe = jnp.exp2(s - m_new)
        l_ref[g] = alpha * l_ref[g] + jnp.sum(pe, axis=-1, keepdims=True)
        acc_ref[g] = alpha * acc_ref[g] + jnp.dot(pe.astype(BF16), v, preferred_element_type=F32)
        m_ref[g] = m_new

    bpt = tk // SEL_BLOCK
    tiles_per_vreg = LANES // bpt
    nblk = A_ref.shape[1]
    def head_rows(pages, g):
        return jnp.concatenate([p[pl.ds(g, PAGE_SIZE, stride=KV_HEADS), :] for p in pages], axis=0).astype(BF16)

    kpos = kt * tk + lax.broadcasted_iota(jnp.int32, (1, tk), 1)
    shift = (LANES - bpt * (kt % tiles_per_vreg)) % LANES
    for g in range(KV_HEADS):
        gs = slice(g * HD, (g + 1) * HD)
        sel = sel_ref[g]
        half = sel[:, 0:LANES]
        for c in range(1, nblk // LANES):
            half = jnp.where(kt // tiles_per_vreg == c, sel[:, c * LANES:(c + 1) * LANES], half)
        moved = pltpu.roll(half, shift, axis=1)
        selx = jnp.dot(moved.astype(BF16), E_ref[...], preferred_element_type=F32)
        ok = (selx > 0.5) & (kpos <= qpos4)
        flash_update(g, _nt(qs[g], head_rows(kpages, g)), ok, head_rows(vpages, g))

    @pl.when(kt == nkt - 1)
    def _():
        n_past = past // SEL_BLOCK
        lane = lax.broadcasted_iota(jnp.int32, (1, LANES), 1)
        gt = jax.nn.sigmoid(z_gate[...])
        kw_new = z_kw[...]
        vw_new = z_vw[...]
        nk = KV_HEADS
        kwo_ref[0:nk * (wb - T), :] = kwin_ref[nk * T:nk * wb, :]
        vwo_ref[0:nk * (wb - T), :] = vwin_ref[nk * T:nk * wb, :]
        for g in range(nk):
            kwo_ref[pl.ds(nk * (wb - T) + g, T, stride=nk), :] = kw_new[:, g * HD:(g + 1) * HD]
            vwo_ref[pl.ds(nk * (wb - T) + g, T, stride=nk), :] = vw_new[:, g * HD:(g + 1) * HD]

        def padded(rows):
            pad_ref[...] = jnp.zeros_like(pad_ref)
            pad_ref[0:T, :] = rows
            return pad_ref[...].astype(BF16)

        ks_new = padded(z_ks[...])
        vs_new = padded(z_vs[...])
        kw_pad = padded(kw_new)
        vw_pad = padded(vw_new)
        for g in range(KV_HEADS):
            gs = slice(g * HD, (g + 1) * HD)
            q = qs[g]
            sel_tail = sel_ref[g][:, n_past:n_past + 1]
            ok = (sel_tail > 0.5) & (past + lane <= qpos4) & (lane < T)
            flash_update(g, _nt(q, ks_new[:, gs]), ok, vs_new[:, gs])
            o_slc = acc_ref[g] / l_ref[g]
            widx = lax.broadcasted_iota(jnp.int32, (1, wb), 1)
            rel_c = qpos4 - (past - wb + widx)
            ok_c = (rel_c >= 0) & (rel_c < WINDOW) & (past - wb + widx >= 0)
            rel_n = qpos4 - (past + lane)
            ok_n = (rel_n >= 0) & (rel_n < WINDOW) & (lane < T)
            s_c = _nt(q, kwin_ref[pl.ds(g, wb, stride=KV_HEADS), :].astype(BF16))
            s_n = _nt(q, kw_pad[:, gs])
            ew, inv_w = _softmax_parts(jnp.concatenate([s_c, s_n], axis=1),
                                       jnp.concatenate([ok_c, ok_n], axis=1))
            o_win = jnp.dot(ew[:, :wb].astype(BF16), vwin_ref[pl.ds(g, wb, stride=KV_HEADS), :].astype(BF16),
                            preferred_element_type=F32)
            o_win = (o_win + jnp.dot(ew[:, wb:].astype(BF16), vw_pad[:, gs], preferred_element_type=F32)) * inv_w
            o_cmp = ocmp_ref[g]
            for r in range(GROUP):
                rs = slice(r * T, (r + 1) * T)
                c0 = (g * GROUP + r) * 3
                o = (gt[:, c0:c0 + 1] * o_cmp[rs] + gt[:, c0 + 1:c0 + 2] * o_slc[rs]
                     + gt[:, c0 + 2:c0 + 3] * o_win[rs])
                o_ref[:, (g * GROUP + r) * HD:(g * GROUP + r + 1) * HD] = o


def _nsa_sample(z32, kc, vc, pool_k, pool_v, kwin, vwin, page_table, A, E, *, B, T, G):
    NP = page_table.shape[1]
    past = NP * PAGE_SIZE
    wb = kwin.shape[0] // (B * KV_HEADS)
    ncr = kc.shape[1]
    kvw = KV_HEADS * HD
    R = GROUP * T
    nblk = A.shape[1]

    def page_spec(k):
        return pl.BlockSpec((KV_HEADS * PAGE_SIZE, HD), lambda b, kt, pt: (pt[b, kt * G + k], 0))

    win_spec = pl.BlockSpec((KV_HEADS * wb, HD), lambda b, kt, pt: (b, 0))

    def zcol(width, col):
        return pl.BlockSpec((T, width), lambda b, kt, pt: (b, col * LANES // width))

    const = lambda *shape: pl.BlockSpec(shape, lambda b, kt, pt: (0,) * len(shape))
    per_b = lambda *shape: pl.BlockSpec((1,) + shape, lambda b, kt, pt: (b,) + (0,) * len(shape))
    in_specs = [page_spec(k) for k in range(G)] * 2 + [
        zcol(A_HEADS * HD, COL_Q), zcol(kvw, COL_KS), zcol(kvw, COL_VS), zcol(kvw, COL_KW), zcol(kvw, COL_VW),
        zcol(LANES, COL_SMALL), per_b(ncr, kvw), per_b(ncr, kvw), win_spec, win_spec,
        const(*A.shape), const(*E.shape)]
    return pl.pallas_call(
        functools.partial(_nsa_sample_kernel, G=G, T=T, past=past, wb=wb),
        grid_spec=pltpu.PrefetchScalarGridSpec(
            num_scalar_prefetch=1,
            grid=(B, NP // G),
            in_specs=in_specs,
            out_specs=[pl.BlockSpec((T, A_HEADS * HD), lambda b, kt, pt: (b, 0)), win_spec, win_spec],
            scratch_shapes=[
                pltpu.VMEM((KV_HEADS, R, nblk), F32), pltpu.VMEM((KV_HEADS, R, HD), F32),
                pltpu.VMEM((KV_HEADS, R, 1), F32), pltpu.VMEM((KV_HEADS, R, 1), F32),
                pltpu.VMEM((KV_HEADS, R, HD), F32), pltpu.VMEM((LANES, kvw), F32)],
        ),
        out_shape=[jax.ShapeDtypeStruct((B * T, A_HEADS * HD), F32),
                   jax.ShapeDtypeStruct(kwin.shape, F32), jax.ShapeDtypeStruct(vwin.shape, F32)],
        compiler_params=_cparams("arbitrary", "arbitrary"),
        name="nsa_sample",
    )(page_table, *([pool_k] * G), *([pool_v] * G), z32, z32, z32, z32, z32, z32, kc, vc, kwin, vwin, A, E)


def _sel_matrices(ncr, nblk_pad, tk):
    ratio = SEL_BLOCK // CMP_STRIDE
    lead = CMP_BLOCK // CMP_STRIDE - 1
    c = jnp.arange(ncr)[:, None] - 1
    j = jnp.arange(nblk_pad)[None, :]
    A = ((c >= 0) & (c >= ratio * j - lead) & (c <= ratio * j + ratio - 1)).astype(BF16)
    jj = jnp.arange(LANES)[:, None]
    cc = jnp.arange(tk)[None, :]
    E = ((jj == cc // SEL_BLOCK) & (jj < tk // SEL_BLOCK)).astype(BF16)
    return A, E


def _pack_params(p):
    d_nsa = A_HEADS * HD
    dm = M_HEADS * HD
    q_end = d_nsa
    kv_end = q_end + 6 * KV_HEADS * HD
    g_end = kv_end + 3 * A_HEADS
    qk_end = g_end + 2 * dm
    v_end = qk_end + dm
    if_end = v_end + 2 * M_HEADS
    w_in = p['w_in']
    d_model = w_in.shape[0]
    pad = N_PACKED - (COL_SMALL * LANES + 3 * A_HEADS + 2 * M_HEADS)
    w_packed = jnp.concatenate([
        w_in[:, g_end:qk_end], w_in[:, :q_end], w_in[:, qk_end:v_end], w_in[:, if_end:],
        w_in[:, q_end:kv_end], w_in[:, kv_end:g_end], w_in[:, v_end:if_end],
        jnp.zeros((d_model, pad), w_in.dtype)], axis=1).astype(BF16)
    gain = jnp.ones((N_PACKED,), F32)
    flag = jnp.zeros((N_PACKED,), F32)

    def put(vec, flg, col, g, reps):
        lo = col * LANES
        vec = vec.at[lo:lo + reps * HD].set(jnp.tile(g, reps))
        flg = flg.at[lo:lo + reps * HD].set(1.0)
        return vec, flg

    gain, flag = put(gain, flag, COL_Q, p['g_q'] * (HD ** -0.5 * math.log2(math.e)), A_HEADS)
    gain, flag = put(gain, flag, COL_KS, p['g_ks'], KV_HEADS)
    gain, flag = put(gain, flag, COL_KW, p['g_kw'], KV_HEADS)
    gbias = jnp.zeros((LANES,), F32)
    gbias = gbias.at[IG_LANE:IG_LANE + M_HEADS].set(p['b_i'])
    gbias = gbias.at[FG_LANE:FG_LANE + M_HEADS].set(p['b_f'])
    hrow = jnp.arange(2 * M_HEADS)[:, None]
    lane = jnp.arange(LANES)[None, :]
    sel = jnp.where(hrow < M_HEADS, lane == IG_LANE + hrow, lane == FG_LANE + hrow - M_HEADS).astype(BF16)
    half = CMP_STRIDE * HD

    def w1cat(w1):
        return jnp.concatenate([w1[:half], w1[half:]], axis=1).astype(BF16)

    return dict(
        g_attn=p['g_attn'].reshape(1, -1), w_packed=w_packed, gain=gain.reshape(1, -1), flag=flag.reshape(1, -1),
        w_conv=p['w_conv'], b_conv=p['b_conv'].reshape(1, -1), gbias=gbias.reshape(1, -1), sel=sel,
        g_mnorm=p['g_mnorm'],
        w1cat_k=w1cat(p['w_phi1_k']), w1cat_v=w1cat(p['w_phi1_v']),
        cb_k=_pe_bias(p['pe_k'], p['w_phi1_k']), cb_v=_pe_bias(p['pe_v'], p['w_phi1_v']),
        w2_k=p['w_phi2_k'].astype(BF16), w2_v=p['w_phi2_v'].astype(BF16), g_kc=p['g_kc'].reshape(1, -1),
        wo_a=p['w_out'][:d_nsa].astype(BF16), wo_m=p['w_out'][d_nsa:].astype(BF16),
        g_mlp=p['g_mlp'].reshape(1, -1), w_up=p['w_up'].astype(BF16), w_down=p['w_down'].astype(BF16),
    )


def _kv_cols(z32, col, n):
    return z32[:, col * LANES:(col + n) * LANES]


def _prompt_pass(x, pk):
    T = x.shape[0]
    dm = M_HEADS * HD
    tm = min(T, 1024)
    z32, z16 = _inproj(x, pk['g_attn'], pk['w_packed'], pk['gain'], pk['flag'], tm=tm)
    kc_raw = _kv_cols(z32, COL_KC, KV_HEADS)
    vc_raw = _kv_cols(z32, COL_VC, KV_HEADS)
    ks = _kv_cols(z32, COL_KS, KV_HEADS)
    vs = _kv_cols(z32, COL_VS, KV_HEADS)
    wb = min(WINDOW, T)
    kw = _kv_cols(z32, COL_KW, KV_HEADS)[T - wb:]
    vw = _kv_cols(z32, COL_VW, KV_HEADS)[T - wb:]

    n_pages = T // PAGE_SIZE
    ident = jnp.arange(n_pages, dtype=jnp.int32).reshape(1, n_pages)
    G = min(16, n_pages)
    kc = _compress(kc_raw.reshape(-1, HD), ident, pk['w1cat_k'], pk['cb_k'], pk['w2_k'], pk['g_kc'],
                   normalize=True, G=G)
    vc = _compress(vc_raw.reshape(-1, HD), ident, pk['w1cat_v'], pk['cb_v'], pk['w2_v'], pk['g_kc'],
                   normalize=False, G=G)
    tk = min(T, 1024)
    tq = 256
    nblk_pad = -(-(T // SEL_BLOCK) // LANES) * LANES
    A, E = _sel_matrices(kc.shape[1], nblk_pad, tk)
    vsT = _kv_cols(z16, COL_VS, KV_HEADS).T
    a_out = _nsa_prompt(z32, z16, vsT, kc, vc, A, (2.0 * NEG * E.astype(F32)).astype(BF16).T, T=T, tq=tq, tk=tk)

    L = min(T, 512)
    zeros = lambda *s: jnp.zeros(s, F32)
    m_out, conv, C, n, m = _mlstm(
        z32, z16, zeros(1, CONV_W - 1, 2 * dm), zeros(1, M_HEADS, HD, HD), zeros(1, M_HEADS, HD),
        zeros(1, M_HEADS), pk['w_conv'], pk['b_conv'], pk['gbias'], pk['sel'], pk['g_mnorm'],
        B=1, T=T, L=L, valid=L)
    x1 = _outproj(x, a_out, m_out[0], pk['wo_a'], pk['wo_m'], tm=min(T, 512))
    y = _mlp(x1, pk['g_mlp'], pk['w_up'], pk['w_down'], tm=min(T, 512))
    state = (kc_raw.reshape(1, T, KV_HEADS, HD), vc_raw.reshape(1, T, KV_HEADS, HD),
             ks.reshape(1, T, KV_HEADS, HD), vs.reshape(1, T, KV_HEADS, HD),
             kw.reshape(1, wb, KV_HEADS, HD), vw.reshape(1, wb, KV_HEADS, HD),
             conv, C, n, m.reshape(1, M_HEADS))
    return y, state


def _sample_pass(x, caches, states, page_table, pk):
    B, T, d_model = x.shape
    ck_cmp, cv_cmp, ck_slc, cv_slc, ck_win, cv_win = caches
    s_conv, s_C, s_n, s_m = states
    NP = page_table.shape[1]
    wb = ck_win.shape[1]
    xf = x.reshape(B * T, d_model)
    z32, _ = _inproj(xf, pk['g_attn'], pk['w_packed'], pk['gain'], pk['flag'], tm=B * T)

    assert (NP * PAGE_SIZE + T) // CMP_STRIDE == NP * PAGE_SIZE // CMP_STRIDE
    G = min(32, NP)
    Gc = min(32, NP)
    kc = _compress(ck_cmp.reshape(-1, HD), page_table, pk['w1cat_k'], pk['cb_k'], pk['w2_k'],
                   pk['g_kc'], normalize=True, G=Gc)
    vc = _compress(cv_cmp.reshape(-1, HD), page_table, pk['w1cat_v'], pk['cb_v'], pk['w2_v'],
                   pk['g_kc'], normalize=False, G=Gc)
    n_sel = NP * PAGE_SIZE // SEL_BLOCK + -(-T // SEL_BLOCK)
    nblk_pad = -(-n_sel // LANES) * LANES
    A, E = _sel_matrices(kc.shape[1], nblk_pad, G * PAGE_SIZE)
    a_out, kw_o, vw_o = _nsa_sample(
        z32, kc, vc, ck_slc.reshape(-1, HD), cv_slc.reshape(-1, HD),
        ck_win.reshape(-1, HD), cv_win.reshape(-1, HD), page_table, A, E, B=B, T=T, G=G)

    m_out, conv, C, n, m = _mlstm(
        z32, z32, s_conv, s_C, s_n, s_m, pk['w_conv'], pk['b_conv'], pk['gbias'], pk['sel'], pk['g_mnorm'],
        B=B, T=T, L=LANES, valid=T)
    x1 = _outproj(xf, a_out, m_out.reshape(B * T, -1), pk['wo_a'], pk['wo_m'], tm=B * T)
    y = _mlp(x1, pk['g_mlp'], pk['w_up'], pk['w_down'], tm=B * T)
    rows = lambda col: _kv_cols(z32, col, KV_HEADS).reshape(B, T, KV_HEADS, HD)
    state = (rows(COL_KC), rows(COL_VC), rows(COL_KS), rows(COL_VS),
             kw_o.reshape(B, wb, KV_HEADS, HD), vw_o.reshape(B, wb, KV_HEADS, HD),
             conv, C, n, m.reshape(B, M_HEADS))
    return y.reshape(B, T, d_model), state


_PARAM_NAMES = ('g_attn', 'w_in', 'w_conv', 'b_conv', 'b_i', 'b_f', 'g_mnorm', 'g_q', 'g_ks', 'g_kw', 'g_kc',
                'w_phi1_k', 'w_phi2_k', 'pe_k', 'w_phi1_v', 'w_phi2_v', 'pe_v', 'w_out', 'g_mlp', 'w_up',
                'w_down')


def kernel(x_prompt, x_sample, cache_k_cmp, cache_v_cmp, cache_k_slc, cache_v_slc, cache_k_win, cache_v_win,
           state_conv, state_C, state_n, state_m, page_table, g_attn, w_in, w_conv, b_conv, b_i, b_f, g_mnorm,
           g_q, g_ks, g_kw, g_kc, w_phi1_k, w_phi2_k, pe_k, w_phi1_v, w_phi2_v, pe_v, w_out, g_mlp, w_up,
           w_down):
    weights = (g_attn, w_in, w_conv, b_conv, b_i, b_f, g_mnorm, g_q, g_ks, g_kw, g_kc, w_phi1_k, w_phi2_k,
               pe_k, w_phi1_v, w_phi2_v, pe_v, w_out, g_mlp, w_up, w_down)
    depth = w_in.shape[0]
    assert depth == 1 and x_prompt.shape[0] == 1
    pk = _pack_params({k: w[0] for k, w in zip(_PARAM_NAMES, weights)})
    yp, st_p = _prompt_pass(x_prompt[0], pk)
    ys, st_s = _sample_pass(
        x_sample, (cache_k_cmp[0], cache_v_cmp[0], cache_k_slc[0], cache_v_slc[0], cache_k_win[0], cache_v_win[0]),
        (state_conv[0], state_C[0], state_n[0], state_m[0]), page_table, pk)
    return (yp[None], ys) + tuple(s[None] for s in st_p) + tuple(s[None] for s in st_s)
```

```python
import functools
import math

import jax
import jax.numpy as jnp
from jax import lax
from jax.experimental import pallas as pl
from jax.experimental.pallas import tpu as pltpu

F32 = jnp.float32
BF16 = jnp.bfloat16

EPS = 1e-6
NEG = -1e30
BIG = 1e9

M_HEADS = 8
A_HEADS = 8
KV_HEADS = 2
GROUP = A_HEADS // KV_HEADS
HD = 128
CONV_W = 4
CMP_BLOCK = 32
CMP_STRIDE = 16
CMP_HIDDEN = 2 * HD
SEL_BLOCK = 64
N_SELECT = 16
WINDOW = 512
PAGE_SIZE = 128

LANES = 128
VMEM_LIMIT_BYTES = 56 * 1024 * 1024

COL_MQK = 0
COL_Q = 16
COL_MV = 24
COL_MO = 32
COL_KC = 40
COL_VC = 42
COL_KS = 44
COL_VS = 46
COL_KW = 48
COL_VW = 50
COL_SMALL = 52
N_PACKED = 56 * LANES
GATE_LANE = 0
IG_LANE = 24
FG_LANE = 32


def _cparams(*sem):
    return pltpu.CompilerParams(dimension_semantics=sem, vmem_limit_bytes=VMEM_LIMIT_BYTES)


def _nt(a, b):
    return lax.dot_general(a, b, (((1,), (1,)), ((), ())), preferred_element_type=F32)


def _split3(x):
    x1 = x.astype(BF16)
    r1 = x - x1.astype(F32)
    x2 = r1.astype(BF16)
    r2 = r1 - x2.astype(F32)
    return x1, x2, r2.astype(BF16)


def _inproj_kernel(x_ref, g_ref, w_ref, gain_ref, flag_ref, z32_ref, z16_ref, h_ref, *, norm_tiles):
    j = pl.program_id(1)

    @pl.when(j == 0)
    def _():
        x = x_ref[...]
        ms = jnp.mean(x * x, axis=-1, keepdims=True)
        h_ref[...] = (x * lax.rsqrt(ms + EPS) * g_ref[...]).astype(BF16)

    is_norm = functools.reduce(jnp.logical_or, [j == t for t in norm_tiles])
    tn = w_ref.shape[1]
    chunk = 2 * LANES

    @pl.when(is_norm)
    def _():
        for c0 in range(0, tn, chunk):
            z = jnp.dot(h_ref[...], w_ref[:, c0:c0 + chunk], preferred_element_type=F32)
            for c in range(c0, c0 + chunk, LANES):
                sl = slice(c, c + LANES)
                zc = z[:, c - c0:c - c0 + LANES]
                nrm = zc * lax.rsqrt(jnp.mean(zc * zc, axis=-1, keepdims=True) + EPS) * gain_ref[:, sl]
                out = jnp.where(flag_ref[:, sl] > 0.5, nrm, zc)
                z32_ref[:, sl] = out
                z16_ref[:, sl] = out.astype(BF16)

    @pl.when(jnp.logical_not(is_norm))
    def _():
        for c0 in range(0, tn, chunk):
            z = jnp.dot(h_ref[...], w_ref[:, c0:c0 + chunk], preferred_element_type=F32)
            z32_ref[:, c0:c0 + chunk] = z
            z16_ref[:, c0:c0 + chunk] = z.astype(BF16)


def _inproj(x, g, w, gain, flag, *, tm, tn=1024):
    M, K = x.shape
    N = w.shape[1]
    norm_tiles = tuple(sorted({c * LANES // tn for c in (COL_Q, COL_Q + A_HEADS - 1, COL_KS, COL_KW)}))
    return pl.pallas_call(
        functools.partial(_inproj_kernel, norm_tiles=norm_tiles),
        grid=(M // tm, N // tn),
        in_specs=[
            pl.BlockSpec((tm, K), lambda i, j: (i, 0)),
            pl.BlockSpec((1, K), lambda i, j: (0, 0)),
            pl.BlockSpec((K, tn), lambda i, j: (0, j)),
            pl.BlockSpec((1, tn), lambda i, j: (0, j)),
            pl.BlockSpec((1, tn), lambda i, j: (0, j)),
        ],
        out_specs=[
            pl.BlockSpec((tm, tn), lambda i, j: (i, j)),
            pl.BlockSpec((tm, tn), lambda i, j: (i, j)),
        ],
        out_shape=[jax.ShapeDtypeStruct((M, N), F32), jax.ShapeDtypeStruct((M, N), BF16)],
        scratch_shapes=[pltpu.VMEM((tm, K), BF16)],
        compiler_params=_cparams("arbitrary", "arbitrary"),
        name="inproj",
    )(x, g, w, gain, flag)


def _outproj_kernel(x_ref, a_ref, m_ref, wa_ref, wm_ref, o_ref):
    acc = jnp.dot(a_ref[...].astype(BF16), wa_ref[...], preferred_element_type=F32)
    acc = acc + jnp.dot(m_ref[...].astype(BF16), wm_ref[...], preferred_element_type=F32)
    o_ref[...] = x_ref[...] + acc


def _outproj(x, a, m, wa, wm, *, tm, tn=2048):
    M, N = x.shape
    Ka, Km = a.shape[1], m.shape[1]
    return pl.pallas_call(
        _outproj_kernel,
        grid=(M // tm, N // tn),
        in_specs=[
            pl.BlockSpec((tm, tn), lambda i, j: (i, j)),
            pl.BlockSpec((tm, Ka), lambda i, j: (i, 0)),
            pl.BlockSpec((tm, Km), lambda i, j: (i, 0)),
            pl.BlockSpec((Ka, tn), lambda i, j: (0, j)),
            pl.BlockSpec((Km, tn), lambda i, j: (0, j)),
        ],
        out_specs=pl.BlockSpec((tm, tn), lambda i, j: (i, j)),
        out_shape=jax.ShapeDtypeStruct((M, N), F32),
        compiler_params=_cparams("arbitrary", "arbitrary"),
        name="outproj",
    )(x, a, m, wa, wm)


def _mlp_kernel(x_ref, g_ref, wu_ref, wd_ref, o_ref, h_ref, acc_ref):
    k = pl.program_id(1)

    @pl.when(k == 0)
    def _():
        x = x_ref[...]
        ms = jnp.mean(x * x, axis=-1, keepdims=True)
        h_ref[...] = (x * lax.rsqrt(ms + EPS) * g_ref[...]).astype(BF16)
        acc_ref[...] = jnp.zeros_like(acc_ref)

    u = jnp.dot(h_ref[...], wu_ref[...], preferred_element_type=F32)
    u = jnp.square(jnp.maximum(u, 0.0)).astype(BF16)
    acc_ref[...] += jnp.dot(u, wd_ref[...], preferred_element_type=F32)

    @pl.when(k == pl.num_programs(1) - 1)
    def _():
        o_ref[...] = x_ref[...] + acc_ref[...]


def _mlp(x, g, wu, wd, *, tm, tf=1024):
    M, D = x.shape
    FF = wu.shape[1]
    return pl.pallas_call(
        _mlp_kernel,
        grid=(M // tm, FF // tf),
        in_specs=[
            pl.BlockSpec((tm, D), lambda i, k: (i, 0)),
            pl.BlockSpec((1, D), lambda i, k: (0, 0)),
            pl.BlockSpec((D, tf), lambda i, k: (0, k)),
            pl.BlockSpec((tf, D), lambda i, k: (k, 0)),
        ],
        out_specs=pl.BlockSpec((tm, D), lambda i, k: (i, 0)),
        out_shape=jax.ShapeDtypeStruct((M, D), F32),
        scratch_shapes=[pltpu.VMEM((tm, D), BF16), pltpu.VMEM((tm, D), F32)],
        compiler_params=_cparams("arbitrary", "arbitrary"),
        name="mlp",
    )(x, g, wu, wd)


def _silu(x):
    return x * jax.nn.sigmoid(x)


def _log_sigmoid(x):
    return jnp.minimum(x, 0.0) - jnp.log1p(jnp.exp(-jnp.abs(x)))


def _mlstm_kernel(mqk_ref, mv_ref, sm_ref, mo_ref, conv0_ref, C0_ref, n0_ref, m0_ref,
                  wconv_ref, bconv_ref, gbias_ref, sel_ref, gm_ref,
                  out_ref, conv_ref, C_ref, n_ref, m_ref,
                  xbuf_ref, *pad_refs, L, valid):
    c = pl.program_id(1)
    dm = M_HEADS * HD

    @pl.when(c == 0)
    def _():
        C_ref[...] = C0_ref[...]
        n_ref[...] = n0_ref[...]
        m_ref[...] = m0_ref[...]
        xbuf_ref[...] = jnp.zeros_like(xbuf_ref)
        xbuf_ref[8 - (CONV_W - 1):8, :] = conv0_ref[0]

    if valid < L:
        vbuf, sbuf, obuf = pad_refs
        vbuf[...] = jnp.zeros_like(vbuf)
        sbuf[...] = jnp.zeros_like(sbuf)
        obuf[...] = jnp.zeros_like(obuf)
        vbuf[0:valid, :] = mv_ref[0]
        sbuf[0:valid, :] = sm_ref[0]
        obuf[0:valid, :] = mo_ref[0]
        xbuf_ref[8:8 + valid, :] = mqk_ref[0]
        v_all, sm, mo = vbuf[...], sbuf[...], obuf[...]
    else:
        xbuf_ref[8:8 + L, :] = mqk_ref[0]
        v_all, sm, mo = mv_ref[0], sm_ref[0], mo_ref[0]

    w = wconv_ref[...]
    y = bconv_ref[...] + xbuf_ref[8:8 + L, :] * w[CONV_W - 1:CONV_W, :]
    for j in range(CONV_W - 1):
        y = y + xbuf_ref[5 + j:5 + j + L, :] * w[j:j + 1, :]
    qk = _silu(y)
    tail = xbuf_ref[5 + valid:8 + valid, :]
    conv_ref[0] = tail
    xbuf_ref[5:8, :] = tail

    row = lax.broadcasted_iota(jnp.int32, (L, 1), 0)
    live = row < valid
    gt = sm + gbias_ref[...]
    ig_all = jnp.where(live, gt, NEG)
    lf_all = jnp.where(live, _log_sigmoid(gt), 0.0)
    F_all = lf_all
    sh = 1
    while sh < L:
        F_all = F_all + jnp.where(row >= sh, pltpu.roll(F_all, sh, axis=0), 0.0)
        sh *= 2
    sel = sel_ref[...]
    head_row = lax.broadcasted_iota(jnp.int32, (2 * M_HEADS, 1), 0)
    rows = jnp.zeros((2 * M_HEADS, L), F32)
    for a, b in zip(_split3(ig_all), _split3(F_all)):
        rows = rows + jnp.where(head_row < M_HEADS, _nt(sel, a), _nt(sel, b))

    tt = lax.broadcasted_iota(jnp.int32, (L, L), 0)
    ss = lax.broadcasted_iota(jnp.int32, (L, L), 1)
    tri = ss <= tt
    for h in range(M_HEADS):
        hs = slice(h * HD, (h + 1) * HD)
        F_c = F_all[:, FG_LANE + h:FG_LANE + h + 1]
        ig_c = ig_all[:, IG_LANE + h:IG_LANE + h + 1]
        ig_r = rows[h:h + 1, :]
        F_r = rows[M_HEADS + h:M_HEADS + h + 1, :]
        m_h = m_ref[0, :, h:h + 1]
        D = jnp.where(tri, ig_r + F_c - F_r, NEG)
        a_c = m_h + F_c
        mt = jnp.maximum(a_c, jnp.max(D, axis=-1, keepdims=True))
        qf = qk[:, hs]
        kf = qk[:, dm + h * HD:dm + (h + 1) * HD] * (HD ** -0.5)
        qb, kb = qf.astype(BF16), kf.astype(BF16)
        vb = v_all[:, hs].astype(BF16)
        S = _nt(qb, kb) * jnp.exp(D - mt)
        inter = jnp.exp(a_c - mt)
        Ch = C_ref[0, h]
        nh = n_ref[0, h:h + 1, :]
        num = jnp.dot(S.astype(BF16), vb, preferred_element_type=F32) + inter * jnp.dot(
            qb, Ch.astype(BF16), preferred_element_type=F32)
        den = jnp.sum(S, axis=-1, keepdims=True) + inter * jnp.sum(qf * nh, axis=-1, keepdims=True)
        hh = num / jnp.maximum(jnp.abs(den), jnp.exp(-mt))
        m_new = mt[valid - 1:valid, :]
        F_last = F_c[valid - 1:valid, :]
        w_c = jnp.exp(ig_c + F_last - F_c - m_new)
        decay = jnp.exp(a_c[valid - 1:valid, :] - m_new)
        kw = kf * w_c
        C_ref[0, h] = decay * Ch + lax.dot_general(
            kw.astype(BF16), vb, (((0,), (0,)), ((), ())), preferred_element_type=F32)
        n_ref[0, h:h + 1, :] = decay * nh + jnp.sum(kw, axis=0, keepdims=True)
        m_ref[0, :, h:h + 1] = m_new
        hn = hh * lax.rsqrt(jnp.mean(hh * hh, axis=-1, keepdims=True) + EPS) * gm_ref[h:h + 1, :]
        res = (hn * jax.nn.sigmoid(mo[:, hs])).astype(out_ref.dtype)
        out_ref[0, :, hs] = res[0:valid, :] if valid < L else res


def _mlstm(z32, z16, conv0, C0, n0, m0, wconv, bconv, gbias, sel, gm, *, B, T, L, valid):
    dm = M_HEADS * HD
    rows = valid if valid < L else L
    nchunk = T // rows
    z32 = z32.reshape(B, T, N_PACKED)
    z16 = z16.reshape(B, T, N_PACKED)
    scratch = [pltpu.VMEM((L + 8, 2 * dm), F32)]
    if valid < L:
        scratch += [pltpu.VMEM((L, dm), F32), pltpu.VMEM((L, LANES), F32), pltpu.VMEM((L, dm), F32)]
    zv = z32 if valid < L else z16
    out_dtype = F32 if valid < L else BF16
    const = lambda *shape: pl.BlockSpec(shape, lambda b, c: (0,) * len(shape))
    return pl.pallas_call(
        functools.partial(_mlstm_kernel, L=L, valid=valid),
        grid=(B, nchunk),
        in_specs=[
            pl.BlockSpec((1, rows, 2 * dm), lambda b, c: (b, c, COL_MQK * LANES // (2 * dm))),
            pl.BlockSpec((1, rows, dm), lambda b, c: (b, c, COL_MV * LANES // dm)),
            pl.BlockSpec((1, rows, LANES), lambda b, c: (b, c, COL_SMALL)),
            pl.BlockSpec((1, rows, dm), lambda b, c: (b, c, COL_MO * LANES // dm)),
            pl.BlockSpec((1, CONV_W - 1, 2 * dm), lambda b, c: (b, 0, 0)),
            pl.BlockSpec((1, M_HEADS, HD, HD), lambda b, c: (b, 0, 0, 0)),
            pl.BlockSpec((1, M_HEADS, HD), lambda b, c: (b, 0, 0)),
            pl.BlockSpec((1, 1, M_HEADS), lambda b, c: (b, 0, 0)),
            const(CONV_W, 2 * dm), const(1, 2 * dm), const(1, LANES), const(2 * M_HEADS, LANES),
            const(M_HEADS, HD),
        ],
        out_specs=[
            pl.BlockSpec((1, rows, dm), lambda b, c: (b, c, 0)),
            pl.BlockSpec((1, CONV_W - 1, 2 * dm), lambda b, c: (b, 0, 0)),
            pl.BlockSpec((1, M_HEADS, HD, HD), lambda b, c: (b, 0, 0, 0)),
            pl.BlockSpec((1, M_HEADS, HD), lambda b, c: (b, 0, 0)),
            pl.BlockSpec((1, 1, M_HEADS), lambda b, c: (b, 0, 0)),
        ],
        out_shape=[
            jax.ShapeDtypeStruct((B, T, dm), out_dtype),
            jax.ShapeDtypeStruct((B, CONV_W - 1, 2 * dm), F32),
            jax.ShapeDtypeStruct((B, M_HEADS, HD, HD), F32),
            jax.ShapeDtypeStruct((B, M_HEADS, HD), F32),
            jax.ShapeDtypeStruct((B, 1, M_HEADS), F32),
        ],
        scratch_shapes=scratch,
        compiler_params=_cparams("arbitrary", "arbitrary"),
        name="mlstm",
    )(z32, zv, z32, z32, conv0, C0, n0, m0.reshape(B, 1, M_HEADS), wconv, bconv, gbias, sel, gm)


_PIECE_PITCH = KV_HEADS * CMP_STRIDE + 1


def _compress_kernel(pt_ref, *refs, G, normalize):
    pages = refs[:G]
    w1_ref, cb_ref, w2_ref, gain_ref, out_ref, carry_ref, pad_ref = refs[G:]
    pg = pl.program_id(1)
    ppp = PAGE_SIZE // CMP_STRIDE
    R = G * ppp

    @pl.when(pg == 0)
    def _():
        carry_ref[...] = jnp.zeros_like(carry_ref)

    piece = KV_HEADS * CMP_STRIDE
    for k, p in enumerate(pages):
        for q in range(ppp):
            pad_ref[pl.ds((k * ppp + q) * _PIECE_PITCH, piece), :] = p[q * piece:(q + 1) * piece, :]
    xs = []
    for g in range(KV_HEADS):
        for k in range(G):
            xs.append(jnp.concatenate(
                [pad_ref[pl.ds(k * ppp * _PIECE_PITCH + KV_HEADS * j + g, ppp, stride=_PIECE_PITCH), :]
                 for j in range(CMP_STRIDE)], axis=1))
    X2 = jnp.concatenate(xs, axis=0).astype(BF16)
    LH = jnp.dot(X2, w1_ref[...], preferred_element_type=F32)
    row = lax.broadcasted_iota(jnp.int32, (R, 1), 0)
    for g in range(KV_HEADS):
        lo = LH[g * R:(g + 1) * R, :CMP_HIDDEN]
        hi = LH[g * R:(g + 1) * R, CMP_HIDDEN:]
        lo_prev = jnp.where(row == 0, carry_ref[g:g + 1, :], pltpu.roll(lo, 1, axis=0))
        carry_ref[g:g + 1, :] = lo[R - 1:R, :]
        hid = lo_prev + hi + cb_ref[0:1, :]
        out = jnp.dot(_silu(hid).astype(BF16), w2_ref[...], preferred_element_type=F32)
        if normalize:
            out = out * lax.rsqrt(jnp.mean(out * out, axis=-1, keepdims=True) + EPS) * gain_ref[...]
        out_ref[0, :, g * HD:(g + 1) * HD] = out.astype(out_ref.dtype)


def _compress(pool, page_table, w1cat, cbias, w2, gain, *, normalize, G=16):
    B, NP = page_table.shape
    ppp = PAGE_SIZE // CMP_STRIDE
    R = G * ppp

    def page_spec(k):
        return pl.BlockSpec((KV_HEADS * PAGE_SIZE, HD), lambda b, pg, pt: (pt[b, pg * G + k], 0))

    const = lambda *shape: pl.BlockSpec(shape, lambda b, pg, pt: (0,) * len(shape))
    return pl.pallas_call(
        functools.partial(_compress_kernel, G=G, normalize=normalize),
        grid_spec=pltpu.PrefetchScalarGridSpec(
            num_scalar_prefetch=1,
            grid=(B, NP // G),
            in_specs=[page_spec(k) for k in range(G)] + [
                const(CMP_STRIDE * HD, 2 * CMP_HIDDEN), const(8, CMP_HIDDEN), const(CMP_HIDDEN, HD),
                const(1, HD)],
            out_specs=pl.BlockSpec((1, R, KV_HEADS * HD), lambda b, pg, pt: (b, pg, 0)),
            scratch_shapes=[pltpu.VMEM((8, CMP_HIDDEN), F32),
                            pltpu.VMEM((-(-G * ppp * _PIECE_PITCH // 8) * 8, HD), F32)],
        ),
        out_shape=jax.ShapeDtypeStruct((B, NP * ppp, KV_HEADS * HD), BF16),
        compiler_params=_cparams("arbitrary", "arbitrary"),
        name="compress",
    )(page_table, *([pool] * G), w1cat, cbias, w2, gain)


def _pe_bias_kernel(pe_ref, w1_ref, o_ref):
    o_ref[...] = jnp.dot(pe_ref[...].astype(BF16), w1_ref[...].astype(BF16), preferred_element_type=F32)


def _pe_bias(pe, w1):
    pe8 = jnp.broadcast_to(pe.reshape(1, -1), (8, pe.size))
    return pl.pallas_call(
        _pe_bias_kernel,
        out_shape=jax.ShapeDtypeStruct((8, w1.shape[1]), F32),
        compiler_params=pltpu.CompilerParams(vmem_limit_bytes=VMEM_LIMIT_BYTES),
        name="pe_bias",
    )(pe8, w1)


def _softmax_parts(s, ok):
    sm = jnp.where(ok, s, NEG)
    mx = jnp.max(sm, axis=-1, keepdims=True)
    e = jnp.exp2(sm - mx)
    den = jnp.sum(e, axis=-1, keepdims=True)
    return e, jnp.where(mx > 0.5 * NEG, 1.0 / den, 0.0)


def _topk_mask(score, blk, n_pick):
    width = float(score.shape[-1])
    blkf = blk.astype(F32)
    sel = jnp.zeros(score.shape, F32)
    for _ in range(n_pick):
        mx = jnp.max(score, axis=-1, keepdims=True)
        idx = jnp.min(jnp.where(score == mx, blkf, width), axis=-1, keepdims=True)
        hit = blkf == idx
        sel = jnp.where(hit, 1.0, sel)
        score = jnp.where(hit, -jnp.inf, score)
    return sel


def _topk_mask_cols(score_t, n_pick):
    width = float(score_t.shape[0])
    blkf = lax.broadcasted_iota(jnp.int32, (score_t.shape[0], 1), 0).astype(F32)
    sel = jnp.zeros(score_t.shape, F32)
    for _ in range(n_pick):
        mx = jnp.max(score_t, axis=0, keepdims=True)
        idx = jnp.min(jnp.where(score_t == mx, blkf, width), axis=0, keepdims=True)
        hit = blkf == idx
        sel = jnp.where(hit, 1.0, sel)
        score_t = jnp.where(hit, -jnp.inf, score_t)
    return sel


def _stack_heads(qf):
    return jnp.concatenate([qf[:, r * HD:(r + 1) * HD] for r in range(GROUP)], axis=0)


def _tile_rows(x):
    return jnp.concatenate([x] * GROUP, axis=0)


_CMP_COLS_STEP = 256
def _nsa_prompt_kernel(q_ref, kc_ref, vc_ref, ks_ref, vs_ref, *rest, tq, tk, nwin):
    kw_refs = rest[:nwin]
    vw_refs = rest[nwin:2 * nwin]
    gate_ref, A_ref, E_ref, o_ref = rest[2 * nwin:]
    g = pl.program_id(0)
    i = pl.program_id(1)
    q0 = i * tq
    q = _stack_heads(q_ref[...])
    qpos = q0 + lax.broadcasted_iota(jnp.int32, (tq, 1), 0)
    qpos4 = _tile_rows(qpos)

    ncr = kc_ref.shape[1]
    nblk = A_ref.shape[1]

    def cmp_branch(ncols):
        cidx = lax.broadcasted_iota(jnp.int32, (1, ncols), 1)
        cok = (cidx >= 1) & (CMP_STRIDE * cidx + (CMP_BLOCK - CMP_STRIDE - 1) <= qpos4)
        e, inv = _softmax_parts(_nt(q, kc_ref[0, 0:ncols, :]), cok)
        o = jnp.dot(e.astype(BF16), vc_ref[0, 0:ncols, :], preferred_element_type=F32) * inv
        psum = e[0:tq] * inv[0:tq]
        for r in range(1, GROUP):
            psum = psum + e[r * tq:(r + 1) * tq] * inv[r * tq:(r + 1) * tq]
        acc = jnp.zeros((tq, nblk), F32)
        for part in _split3(psum):
            acc = acc + jnp.dot(part, A_ref[0:ncols, :], preferred_element_type=F32)
        return o, acc

    step = _CMP_COLS_STEP if ncr % _CMP_COLS_STEP == 0 else ncr
    need_idx = ((q0 + tq) // CMP_STRIDE - 1) // step

    def pick(k):
        if (k + 1) * step >= ncr:
            return cmp_branch(ncr)
        return lax.cond(need_idx <= k, lambda: cmp_branch((k + 1) * step), lambda: pick(k + 1))

    o_cmp, imp = pick(0)
    blk = lax.broadcasted_iota(jnp.int32, (1, nblk), 1)
    cur = qpos // SEL_BLOCK
    forced = (blk == 0) | (blk == cur) | (blk == cur - 1)
    score = jnp.where(blk <= cur, jnp.where(forced, BIG, imp), NEG)
    s_w, ok_w = [], []
    for j in range(nwin):
        kwpos = q0 + (j - (nwin - 1)) * tq + lax.broadcasted_iota(jnp.int32, (1, tq), 1)
        rel = qpos4 - kwpos
        ok_w.append((rel >= 0) & (rel < WINDOW) & (kwpos >= 0))
        s_w.append(_nt(q, kw_refs[j][...]))
    ew, inv_w = _softmax_parts(jnp.concatenate(s_w, axis=1), jnp.concatenate(ok_w, axis=1))
    o_win = jnp.zeros((GROUP * tq, HD), F32)
    for j in range(nwin):
        o_win = o_win + jnp.dot(ew[:, j * tq:(j + 1) * tq].astype(BF16), vw_refs[j][...],
                                preferred_element_type=F32)
    o_win = o_win * inv_w

    notsel = 1.0 - _topk_mask_cols(score.T, N_SELECT).T

    bpt = tk // SEL_BLOCK
    tiles_per_vreg = LANES // bpt
    rows_q = GROUP * tq
    kpos_col = lax.broadcasted_iota(jnp.int32, (tk, 1), 0)
    qpos_row = q0 + lax.broadcasted_iota(jnp.int32, (1, rows_q), 1) % tq

    def body(kt, carry, causal):
        m_i, l_i, acc = carry
        start = pl.multiple_of(kt * tk, tk)
        half = notsel[:, 0:LANES]
        for c in range(1, nblk // LANES):
            half = jnp.where(kt // tiles_per_vreg == c, notsel[:, c * LANES:(c + 1) * LANES], half)
        shift = (LANES - bpt * (kt % tiles_per_vreg)) % LANES
        moved = pltpu.roll(half, shift, axis=1).astype(BF16)
        qa = jnp.concatenate([q, _tile_rows(moved)], axis=1)
        ka = jnp.concatenate([ks_ref[pl.ds(start, tk), :], E_ref[...]], axis=1)
        s = _nt(ka, qa)
        if causal:
            s = jnp.where(start + kpos_col <= qpos_row, s, 2.0 * NEG)
        m_new = jnp.maximum(m_i, jnp.max(s, axis=0, keepdims=True))
        alpha = jnp.exp2(m_i - m_new)
        pe = jnp.exp2(s - m_new)
        l_new = alpha * l_i + jnp.sum(pe, axis=0, keepdims=True)
        acc = alpha * acc + jnp.dot(vs_ref[:, pl.ds(start, tk)], pe.astype(BF16), preferred_element_type=F32)
        return m_new, l_new, acc

    nkt = (q0 + tq + tk - 1) // tk
    init = (jnp.full((1, rows_q), NEG, F32), jnp.zeros((1, rows_q), F32), jnp.zeros((HD, rows_q), F32))

    def body2(j, carry):
        return body(2 * j + 1, body(2 * j, carry, False), False)

    n_full = nkt - 1
    carry = lax.fori_loop(0, n_full // 2, body2, init)
    carry = lax.cond(n_full % 2 == 1, lambda c: body(n_full - 1, c, False), lambda c: c, carry)
    _, l_f, acc_f = body(nkt - 1, carry, True)
    o_slc = (acc_f / l_f).T

    gt = jax.nn.sigmoid(gate_ref[...])
    gt = jnp.where(g == 0, gt, pltpu.roll(gt, LANES - 3 * GROUP, axis=1))
    for r in range(GROUP):
        rs = slice(r * tq, (r + 1) * tq)
        o = (gt[:, 3 * r:3 * r + 1] * o_cmp[rs] + gt[:, 3 * r + 1:3 * r + 2] * o_slc[rs]
             + gt[:, 3 * r + 2:3 * r + 3] * o_win[rs])
        o_ref[:, r * HD:(r + 1) * HD] = o.astype(o_ref.dtype)


def _nsa_prompt(z32, z16, vsT, kc, vc, A, E, *, T, tq=128, tk=512):
    nwin = WINDOW // tq + 1
    ncr = kc.shape[1]
    gw = GROUP * HD

    def win_spec(col, j):
        return pl.BlockSpec((tq, HD), lambda g, i: (jnp.maximum(i + j - (nwin - 1), 0), col + g))

    const = lambda *shape: pl.BlockSpec(shape, lambda g, i: (0,) * len(shape))
    in_specs = [
        pl.BlockSpec((tq, gw), lambda g, i: (i, COL_Q * LANES // gw + g)),
        pl.BlockSpec((1, ncr, HD), lambda g, i: (0, 0, g)),
        pl.BlockSpec((1, ncr, HD), lambda g, i: (0, 0, g)),
        pl.BlockSpec((T, HD), lambda g, i: (0, COL_KS + g)),
        pl.BlockSpec((HD, T), lambda g, i: (g, 0)),
    ]
    in_specs += [win_spec(COL_KW, j) for j in range(nwin)]
    in_specs += [win_spec(COL_VW, j) for j in range(nwin)]
    in_specs += [pl.BlockSpec((tq, LANES), lambda g, i: (i, COL_SMALL)), const(*A.shape), const(*E.shape)]
    return pl.pallas_call(
        functools.partial(_nsa_prompt_kernel, tq=tq, tk=tk, nwin=nwin),
        grid=(KV_HEADS, T // tq),
        in_specs=in_specs,
        out_specs=pl.BlockSpec((tq, gw), lambda g, i: (i, g)),
        out_shape=jax.ShapeDtypeStruct((T, KV_HEADS * gw), BF16),
        compiler_params=_cparams("arbitrary", "arbitrary"),
        name="nsa_prompt",
    )(z16, kc, vc, z16, vsT, *([z16] * (2 * nwin)), z32, A, E)


def _nsa_sample_kernel(pt_ref, *refs, G, T, past, wb):
    kpages = refs[:G]
    vpages = refs[G:2 * G]
    (z_q, z_ks, z_vs, z_kw, z_vw, z_gate, kc_ref, vc_ref, kwin_ref, vwin_ref, A_ref, E_ref,
     o_ref, kwo_ref, vwo_ref,
     sel_ref, ocmp_ref, m_ref, l_ref, acc_ref, pad_ref) = refs[2 * G:]
    kt = pl.program_id(1)
    nkt = pl.num_programs(1)
    tk = G * PAGE_SIZE
    R = GROUP * T
    gw = GROUP * HD
    t_col = lax.broadcasted_iota(jnp.int32, (T, 1), 0)
    qpos4 = _tile_rows(past + t_col)
    qall = z_q[...].astype(BF16)
    qs = [_stack_heads(qall[:, g * gw:(g + 1) * gw]) for g in range(KV_HEADS)]

    @pl.when(kt == 0)
    def _():
        nblk = A_ref.shape[1]
        blk = lax.broadcasted_iota(jnp.int32, (1, nblk), 1)
        cur4 = qpos4 // SEL_BLOCK
        forced = (blk == 0) | (blk == cur4) | (blk == cur4 - 1)
        for g in range(KV_HEADS):
            gs = slice(g * HD, (g + 1) * HD)
            kc = kc_ref[0, :, gs]
            ncr = kc.shape[0]
            cidx = lax.broadcasted_iota(jnp.int32, (1, ncr), 1)
            cok = (cidx >= 1) & (CMP_STRIDE * cidx + (CMP_BLOCK - CMP_STRIDE - 1) <= qpos4)
            e, inv = _softmax_parts(_nt(qs[g], kc), cok)
            ocmp_ref[g] = jnp.dot(e.astype(BF16), vc_ref[0, :, gs], preferred_element_type=F32) * inv
            psum = e[0:T] * inv[0:T]
            for r in range(1, GROUP):
                psum = psum + e[r * T:(r + 1) * T] * inv[r * T:(r + 1) * T]
            imp = jnp.zeros((T, nblk), F32)
            for part in _split3(psum):
                imp = imp + jnp.dot(part, A_ref[...], preferred_element_type=F32)
            score = jnp.where(blk <= cur4, jnp.where(forced, BIG, _tile_rows(imp)), NEG)
            sel_ref[g] = _topk_mask(score, blk, N_SELECT)
        m_ref[...] = jnp.full(m_ref.shape, NEG, F32)
        l_ref[...] = jnp.zeros_like(l_ref)
        acc_ref[...] = jnp.zeros_like(acc_ref)

    def flash_update(g, s, ok, v):
        s = jnp.where(ok, s, 2.0 * NEG)
        m_i = m_ref[g]
        m_new = jnp.maximum(m_i, jnp.max(s, axis=-1, keepdims=True))
        alpha = jnp.exp2(m_i - m_new)
        pe = jnp.exp2(s - m_new)
        l_ref[g] = alpha * l_ref[g] + jnp.sum(pe, axis=-1, keepdims=True)
        acc_ref[g] = alpha * acc_ref[g] + jnp.dot(pe.astype(BF16), v, preferred_element_type=F32)
        m_ref[g] = m_new

    bpt = tk // SEL_BLOCK
    tiles_per_vreg = LANES // bpt
    nblk = A_ref.shape[1]
    def head_rows(pages, g):
        return jnp.concatenate([p[pl.ds(g, PAGE_SIZE, stride=KV_HEADS), :] for p in pages], axis=0).astype(BF16)

    kpos = kt * tk + lax.broadcasted_iota(jnp.int32, (1, tk), 1)
    shift = (LANES - bpt * (kt % tiles_per_vreg)) % LANES
    for g in range(KV_HEADS):
        gs = slice(g * HD, (g + 1) * HD)
        sel = sel_ref[g]
        half = sel[:, 0:LANES]
        for c in range(1, nblk // LANES):
            half = jnp.where(kt // tiles_per_vreg == c, sel[:, c * LANES:(c + 1) * LANES], half)
        moved = pltpu.roll(half, shift, axis=1)
        selx = jnp.dot(moved.astype(BF16), E_ref[...], preferred_element_type=F32)
        ok = (selx > 0.5) & (kpos <= qpos4)
        flash_update(g, _nt(qs[g], head_rows(kpages, g)), ok, head_rows(vpages, g))

    @pl.when(kt == nkt - 1)
    def _():
        n_past = past // SEL_BLOCK
        lane = lax.broadcasted_iota(jnp.int32, (1, LANES), 1)
        gt = jax.nn.sigmoid(z_gate[...])
        kw_new = z_kw[...]
        vw_new = z_vw[...]
        nk = KV_HEADS
        kwo_ref[0:nk * (wb - T), :] = kwin_ref[nk * T:nk * wb, :]
        vwo_ref[0:nk * (wb - T), :] = vwin_ref[nk * T:nk * wb, :]
        for g in range(nk):
            kwo_ref[pl.ds(nk * (wb - T) + g, T, stride=nk), :] = kw_new[:, g * HD:(g + 1) * HD]
            vwo_ref[pl.ds(nk * (wb - T) + g, T, stride=nk), :] = vw_new[:, g * HD:(g + 1) * HD]

        def padded(rows):
            pad_ref[...] = jnp.zeros_like(pad_ref)
            pad_ref[0:T, :] = rows
            return pad_ref[...].astype(BF16)

        ks_new = padded(z_ks[...])
        vs_new = padded(z_vs[...])
        kw_pad = padded(kw_new)
        vw_pad = padded(vw_new)
        for g in range(KV_HEADS):
            gs = slice(g * HD, (g + 1) * HD)
            q = qs[g]
            sel_tail = sel_ref[g][:, n_past:n_past + 1]
            ok = (sel_tail > 0.5) & (past + lane <= qpos4) & (lane < T)
            flash_update(g, _nt(q, ks_new[:, gs]), ok, vs_new[:, gs])
            o_slc = acc_ref[g] / l_ref[g]
            widx = lax.broadcasted_iota(jnp.int32, (1, wb), 1)
            rel_c = qpos4 - (past - wb + widx)
            ok_c = (rel_c >= 0) & (rel_c < WINDOW) & (past - wb + widx >= 0)
            rel_n = qpos4 - (past + lane)
            ok_n = (rel_n >= 0) & (rel_n < WINDOW) & (lane < T)
            s_c = _nt(q, kwin_ref[pl.ds(g, wb, stride=KV_HEADS), :].astype(BF16))
            s_n = _nt(q, kw_pad[:, gs])
            ew, inv_w = _softmax_parts(jnp.concatenate([s_c, s_n], axis=1),
                                       jnp.concatenate([ok_c, ok_n], axis=1))
            o_win = jnp.dot(ew[:, :wb].astype(BF16), vwin_ref[pl.ds(g, wb, stride=KV_HEADS), :].astype(BF16),
                            preferred_element_type=F32)
            o_win = (o_win + jnp.dot(ew[:, wb:].astype(BF16), vw_pad[:, gs], preferred_element_type=F32)) * inv_w
            o_cmp = ocmp_ref[g]
            for r in range(GROUP):
                rs = slice(r * T, (r + 1) * T)
                c0 = (g * GROUP + r) * 3
                o = (gt[:, c0:c0 + 1] * o_cmp[rs] + gt[:, c0 + 1:c0 + 2] * o_slc[rs]
                     + gt[:, c0 + 2:c0 + 3] * o_win[rs])
                o_ref[:, (g * GROUP + r) * HD:(g * GROUP + r + 1) * HD] = o


def _nsa_sample(z32, kc, vc, pool_k, pool_v, kwin, vwin, page_table, A, E, *, B, T, G):
    NP = page_table.shape[1]
    past = NP * PAGE_SIZE
    wb = kwin.shape[0] // (B * KV_HEADS)
    ncr = kc.shape[1]
    kvw = KV_HEADS * HD
    R = GROUP * T
    nblk = A.shape[1]

    def page_spec(k):
        return pl.BlockSpec((KV_HEADS * PAGE_SIZE, HD), lambda b, kt, pt: (pt[b, kt * G + k], 0))

    win_spec = pl.BlockSpec((KV_HEADS * wb, HD), lambda b, kt, pt: (b, 0))

    def zcol(width, col):
        return pl.BlockSpec((T, width), lambda b, kt, pt: (b, col * LANES // width))

    const = lambda *shape: pl.BlockSpec(shape, lambda b, kt, pt: (0,) * len(shape))
    per_b = lambda *shape: pl.BlockSpec((1,) + shape, lambda b, kt, pt: (b,) + (0,) * len(shape))
    in_specs = [page_spec(k) for k in range(G)] * 2 + [
        zcol(A_HEADS * HD, COL_Q), zcol(kvw, COL_KS), zcol(kvw, COL_VS), zcol(kvw, COL_KW), zcol(kvw, COL_VW),
        zcol(LANES, COL_SMALL), per_b(ncr, kvw), per_b(ncr, kvw), win_spec, win_spec,
        const(*A.shape), const(*E.shape)]
    return pl.pallas_call(
        functools.partial(_nsa_sample_kernel, G=G, T=T, past=past, wb=wb),
        grid_spec=pltpu.PrefetchScalarGridSpec(
            num_scalar_prefetch=1,
            grid=(B, NP // G),
            in_specs=in_specs,
            out_specs=[pl.BlockSpec((T, A_HEADS * HD), lambda b, kt, pt: (b, 0)), win_spec, win_spec],
            scratch_shapes=[
                pltpu.VMEM((KV_HEADS, R, nblk), F32), pltpu.VMEM((KV_HEADS, R, HD), F32),
                pltpu.VMEM((KV_HEADS, R, 1), F32), pltpu.VMEM((KV_HEADS, R, 1), F32),
                pltpu.VMEM((KV_HEADS, R, HD), F32), pltpu.VMEM((LANES, kvw), F32)],
        ),
        out_shape=[jax.ShapeDtypeStruct((B * T, A_HEADS * HD), F32),
                   jax.ShapeDtypeStruct(kwin.shape, F32), jax.ShapeDtypeStruct(vwin.shape, F32)],
        compiler_params=_cparams("arbitrary", "arbitrary"),
        name="nsa_sample",
    )(page_table, *([pool_k] * G), *([pool_v] * G), z32, z32, z32, z32, z32, z32, kc, vc, kwin, vwin, A, E)


def _sel_matrices(ncr, nblk_pad, tk):
    ratio = SEL_BLOCK // CMP_STRIDE
    lead = CMP_BLOCK // CMP_STRIDE - 1
    c = jnp.arange(ncr)[:, None] - 1
    j = jnp.arange(nblk_pad)[None, :]
    A = ((c >= 0) & (c >= ratio * j - lead) & (c <= ratio * j + ratio - 1)).astype(BF16)
    jj = jnp.arange(LANES)[:, None]
    cc = jnp.arange(tk)[None, :]
    E = ((jj == cc // SEL_BLOCK) & (jj < tk // SEL_BLOCK)).astype(BF16)
    return A, E


def _pack_params(p):
    d_nsa = A_HEADS * HD
    dm = M_HEADS * HD
    q_end = d_nsa
    kv_end = q_end + 6 * KV_HEADS * HD
    g_end = kv_end + 3 * A_HEADS
    qk_end = g_end + 2 * dm
    v_end = qk_end + dm
    if_end = v_end + 2 * M_HEADS
    w_in = p['w_in']
    d_model = w_in.shape[0]
    pad = N_PACKED - (COL_SMALL * LANES + 3 * A_HEADS + 2 * M_HEADS)
    w_packed = jnp.concatenate([
        w_in[:, g_end:qk_end], w_in[:, :q_end], w_in[:, qk_end:v_end], w_in[:, if_end:],
        w_in[:, q_end:kv_end], w_in[:, kv_end:g_end], w_in[:, v_end:if_end],
        jnp.zeros((d_model, pad), w_in.dtype)], axis=1).astype(BF16)
    gain = jnp.ones((N_PACKED,), F32)
    flag = jnp.zeros((N_PACKED,), F32)

    def put(vec, flg, col, g, reps):
        lo = col * LANES
        vec = vec.at[lo:lo + reps * HD].set(jnp.tile(g, reps))
        flg = flg.at[lo:lo + reps * HD].set(1.0)
        return vec, flg

    gain, flag = put(gain, flag, COL_Q, p['g_q'] * (HD ** -0.5 * math.log2(math.e)), A_HEADS)
    gain, flag = put(gain, flag, COL_KS, p['g_ks'], KV_HEADS)
    gain, flag = put(gain, flag, COL_KW, p['g_kw'], KV_HEADS)
    gbias = jnp.zeros((LANES,), F32)
    gbias = gbias.at[IG_LANE:IG_LANE + M_HEADS].set(p['b_i'])
    gbias = gbias.at[FG_LANE:FG_LANE + M_HEADS].set(p['b_f'])
    hrow = jnp.arange(2 * M_HEADS)[:, None]
    lane = jnp.arange(LANES)[None, :]
    sel = jnp.where(hrow < M_HEADS, lane == IG_LANE + hrow, lane == FG_LANE + hrow - M_HEADS).astype(BF16)
    half = CMP_STRIDE * HD

    def w1cat(w1):
        return jnp.concatenate([w1[:half], w1[half:]], axis=1).astype(BF16)

    return dict(
        g_attn=p['g_attn'].reshape(1, -1), w_packed=w_packed, gain=gain.reshape(1, -1), flag=flag.reshape(1, -1),
        w_conv=p['w_conv'], b_conv=p['b_conv'].reshape(1, -1), gbias=gbias.reshape(1, -1), sel=sel,
        g_mnorm=p['g_mnorm'],
        w1cat_k=w1cat(p['w_phi1_k']), w1cat_v=w1cat(p['w_phi1_v']),
        cb_k=_pe_bias(p['pe_k'], p['w_phi1_k']), cb_v=_pe_bias(p['pe_v'], p['w_phi1_v']),
        w2_k=p['w_phi2_k'].astype(BF16), w2_v=p['w_phi2_v'].astype(BF16), g_kc=p['g_kc'].reshape(1, -1),
        wo_a=p['w_out'][:d_nsa].astype(BF16), wo_m=p['w_out'][d_nsa:].astype(BF16),
        g_mlp=p['g_mlp'].reshape(1, -1), w_up=p['w_up'].astype(BF16), w_down=p['w_down'].astype(BF16),
    )


def _kv_cols(z32, col, n):
    return z32[:, col * LANES:(col + n) * LANES]


def _prompt_pass(x, pk):
    T = x.shape[0]
    dm = M_HEADS * HD
    tm = min(T, 1024)
    z32, z16 = _inproj(x, pk['g_attn'], pk['w_packed'], pk['gain'], pk['flag'], tm=tm)
    kc_raw = _kv_cols(z32, COL_KC, KV_HEADS)
    vc_raw = _kv_cols(z32, COL_VC, KV_HEADS)
    ks = _kv_cols(z32, COL_KS, KV_HEADS)
    vs = _kv_cols(z32, COL_VS, KV_HEADS)
    wb = min(WINDOW, T)
    kw = _kv_cols(z32, COL_KW, KV_HEADS)[T - wb:]
    vw = _kv_cols(z32, COL_VW, KV_HEADS)[T - wb:]

    n_pages = T // PAGE_SIZE
    ident = jnp.arange(n_pages, dtype=jnp.int32).reshape(1, n_pages)
    G = min(16, n_pages)
    kc = _compress(kc_raw.reshape(-1, HD), ident, pk['w1cat_k'], pk['cb_k'], pk['w2_k'], pk['g_kc'],
                   normalize=True, G=G)
    vc = _compress(vc_raw.reshape(-1, HD), ident, pk['w1cat_v'], pk['cb_v'], pk['w2_v'], pk['g_kc'],
                   normalize=False, G=G)
    tk = min(T, 1024)
    tq = 256
    nblk_pad = -(-(T // SEL_BLOCK) // LANES) * LANES
    A, E = _sel_matrices(kc.shape[1], nblk_pad, tk)
    vsT = _kv_cols(z16, COL_VS, KV_HEADS).T
    a_out = _nsa_prompt(z32, z16, vsT, kc, vc, A, (2.0 * NEG * E.astype(F32)).astype(BF16).T, T=T, tq=tq, tk=tk)

    L = min(T, 512)
    zeros = lambda *s: jnp.zeros(s, F32)
    m_out, conv, C, n, m = _mlstm(
        z32, z16, zeros(1, CONV_W - 1, 2 * dm), zeros(1, M_HEADS, HD, HD), zeros(1, M_HEADS, HD),
        zeros(1, M_HEADS), pk['w_conv'], pk['b_conv'], pk['gbias'], pk['sel'], pk['g_mnorm'],
        B=1, T=T, L=L, valid=L)
    x1 = _outproj(x, a_out, m_out[0], pk['wo_a'], pk['wo_m'], tm=min(T, 512))
    y = _mlp(x1, pk['g_mlp'], pk['w_up'], pk['w_down'], tm=min(T, 512))
    state = (kc_raw.reshape(1, T, KV_HEADS, HD), vc_raw.reshape(1, T, KV_HEADS, HD),
             ks.reshape(1, T, KV_HEADS, HD), vs.reshape(1, T, KV_HEADS, HD),
             kw.reshape(1, wb, KV_HEADS, HD), vw.reshape(1, wb, KV_HEADS, HD),
             conv, C, n, m.reshape(1, M_HEADS))
    return y, state


def _sample_pass(x, caches, states, page_table, pk):
    B, T, d_model = x.shape
    ck_cmp, cv_cmp, ck_slc, cv_slc, ck_win, cv_win = caches
    s_conv, s_C, s_n, s_m = states
    NP = page_table.shape[1]
    wb = ck_win.shape[1]
    xf = x.reshape(B * T, d_model)
    z32, _ = _inproj(xf, pk['g_attn'], pk['w_packed'], pk['gain'], pk['flag'], tm=B * T)

    assert (NP * PAGE_SIZE + T) // CMP_STRIDE == NP * PAGE_SIZE // CMP_STRIDE
    G = min(32, NP)
    Gc = min(32, NP)
    kc = _compress(ck_cmp.reshape(-1, HD), page_table, pk['w1cat_k'], pk['cb_k'], pk['w2_k'],
                   pk['g_kc'], normalize=True, G=Gc)
    vc = _compress(cv_cmp.reshape(-1, HD), page_table, pk['w1cat_v'], pk['cb_v'], pk['w2_v'],
                   pk['g_kc'], normalize=False, G=Gc)
    n_sel = NP * PAGE_SIZE // SEL_BLOCK + -(-T // SEL_BLOCK)
    nblk_pad = -(-n_sel // LANES) * LANES
    A, E = _sel_matrices(kc.shape[1], nblk_pad, G * PAGE_SIZE)
    a_out, kw_o, vw_o = _nsa_sample(
        z32, kc, vc, ck_slc.reshape(-1, HD), cv_slc.reshape(-1, HD),
        ck_win.reshape(-1, HD), cv_win.reshape(-1, HD), page_table, A, E, B=B, T=T, G=G)

    m_out, conv, C, n, m = _mlstm(
        z32, z32, s_conv, s_C, s_n, s_m, pk['w_conv'], pk['b_conv'], pk['gbias'], pk['sel'], pk['g_mnorm'],
        B=B, T=T, L=LANES, valid=T)
    x1 = _outproj(xf, a_out, m_out.reshape(B * T, -1), pk['wo_a'], pk['wo_m'], tm=B * T)
    y = _mlp(x1, pk['g_mlp'], pk['w_up'], pk['w_down'], tm=B * T)
    rows = lambda col: _kv_cols(z32, col, KV_HEADS).reshape(B, T, KV_HEADS, HD)
    state = (rows(COL_KC), rows(COL_VC), rows(COL_KS), rows(COL_VS),
             kw_o.reshape(B, wb, KV_HEADS, HD), vw_o.reshape(B, wb, KV_HEADS, HD),
             conv, C, n, m.reshape(B, M_HEADS))
    return y.reshape(B, T, d_model), state


_PARAM_NAMES = ('g_attn', 'w_in', 'w_conv', 'b_conv', 'b_i', 'b_f', 'g_mnorm', 'g_q', 'g_ks', 'g_kw', 'g_kc',
                'w_phi1_k', 'w_phi2_k', 'pe_k', 'w_phi1_v', 'w_phi2_v', 'pe_v', 'w_out', 'g_mlp', 'w_up',
                'w_down')


def kernel(x_prompt, x_sample, cache_k_cmp, cache_v_cmp, cache_k_slc, cache_v_slc, cache_k_win, cache_v_win,
           state_conv, state_C, state_n, state_m, page_table, g_attn, w_in, w_conv, b_conv, b_i, b_f, g_mnorm,
           g_q, g_ks, g_kw, g_kc, w_phi1_k, w_phi2_k, pe_k, w_phi1_v, w_phi2_v, pe_v, w_out, g_mlp, w_up,
           w_down):
    weights = (g_attn, w_in, w_conv, b_conv, b_i, b_f, g_mnorm, g_q, g_ks, g_kw, g_kc, w_phi1_k, w_phi2_k,
               pe_k, w_phi1_v, w_phi2_v, pe_v, w_out, g_mlp, w_up, w_down)
    depth = w_in.shape[0]
    assert depth == 1 and x_prompt.shape[0] == 1
    pk = _pack_params({k: w[0] for k, w in zip(_PARAM_NAMES, weights)})
    yp, st_p = _prompt_pass(x_prompt[0], pk)
    ys, st_s = _sample_pass(
        x_sample, (cache_k_cmp[0], cache_v_cmp[0], cache_k_slc[0], cache_v_slc[0], cache_k_win[0], cache_v_win[0]),
        (state_conv[0], state_C[0], state_n[0], state_m[0]), page_table, pk)
    return (yp[None], ys) + tuple(s[None] for s in st_p) + tuple(s[None] for s in st_s)
```
